```python
import jax, jax.numpy as jnp
from jax import lax
import numpy as np

D_MODEL = 4096
BATCH = 32
SEQ = 256
DEPTH = 4
DEC_BATCH = 4
DEC_SEQ = 1024
PAST_LEN = 512

GRID_W = 64
N_MIXERS = 2
N_POOL_LAYERS = (DEPTH + 1) // 2
N_RET_LAYERS = DEPTH // 2
POOL_WINDOWS = (2, 4, 8, 16)
N_POOL_GROUPS = len(POOL_WINDOWS)
POOL_GROUP = D_MODEL // N_POOL_GROUPS
N_HEADS = 16
HEAD_K = D_MODEL // N_HEADS
HEAD_V = 2 * HEAD_K
RET_QK = N_HEADS * HEAD_K
RET_V = N_HEADS * HEAD_V
RET_IN = 2 * RET_QK + 2 * RET_V
CHUNK = 128
ROPE_BASE = 10000.0
D_FF = 11008
CONV_W = 3
EPS = 1e-6

kernel_name = "hybrid_pool_retention_diffusion_step"


def rmsnorm(x, g):
    xf = x.astype(jnp.float32)
    y = xf * lax.rsqrt(jnp.mean(xf * xf, axis=-1, keepdims=True) + EPS)
    return (y * g.astype(jnp.float32)).astype(x.dtype)


def pool_mixer(h, w, scale):
    L = h.shape[1]
    hf = h.astype(jnp.float32)
    cs = jnp.concatenate([jnp.zeros_like(hf[:, :1]), jnp.cumsum(hf, axis=1)], axis=1)
    t = np.arange(L)
    outs = []
    for gi, win in enumerate(POOL_WINDOWS):
        lo = np.clip(t - win // 2, 0, L)
        hi = np.clip(t + win // 2, 0, L)
        sl = slice(gi * POOL_GROUP, (gi + 1) * POOL_GROUP)
        cnt = jnp.asarray((hi - lo).astype(np.float32))[None, :, None]
        mean = (cs[:, hi, sl] - cs[:, lo, sl]) / cnt
        pooled = (mean - hf[:, :, sl]).astype(h.dtype)
        outs.append(jnp.einsum('blc,cd->bld', pooled, w[gi]))
    return jnp.concatenate(outs, axis=-1) * scale


def rope_2d_angles(L):
    rows = L // GRID_W
    row_ids = jnp.repeat(jnp.arange(rows), GRID_W).astype(jnp.float32)
    col_ids = jnp.tile(jnp.arange(GRID_W), rows).astype(jnp.float32)
    n = HEAD_K // 4
    freqs = ROPE_BASE ** (-jnp.arange(n, dtype=jnp.float32) / n)
    return row_ids[:, None] * freqs, col_ids[:, None] * freqs


def _rotate(x, ang):
    x1, x2 = jnp.split(x, 2, axis=-1)
    c = jnp.cos(ang).astype(x.dtype)
    s = jnp.sin(ang).astype(x.dtype)
    return jnp.concatenate([x1 * c - x2 * s, x2 * c + x1 * s], axis=-1)


def apply_rope_2d(x, angs):
    xr, xc = jnp.split(x, 2, axis=-1)
    return jnp.concatenate([_rotate(xr, angs[0]), _rotate(xc, angs[1])], axis=-1)


def retention_scan(q, k, v, log_gamma, s0):
    B, H, L, _ = q.shape
    nc = L // CHUNK

    def to_chunks(a):
        return a.reshape(B, H, nc, CHUNK, a.shape[-1]).transpose(2, 0, 1, 3, 4)

    idx = jnp.arange(CHUNK, dtype=jnp.float32)
    diff = idx[:, None] - idx[None, :]
    lg = log_gamma[:, None, None]
    intra_decay = jnp.where(diff >= 0, jnp.exp(lg * jnp.maximum(diff, 0.0)), 0.0)
    q_decay = jnp.exp(log_gamma[:, None] * (idx + 1.0))[..., None]
    k_decay = jnp.exp(log_gamma[:, None] * (CHUNK - 1.0 - idx))[..., None]
    chunk_decay = jnp.exp(log_gamma * CHUNK)[:, None, None]

    def step(s, qkv):
        qc, kc, vc = qkv
        scores = jnp.einsum('bhid,bhjd->bhij', qc, kc) * intra_decay
        o = (jnp.einsum('bhij,bhjv->bhiv', scores, vc)
             + jnp.einsum('bhid,bhdv->bhiv', qc, s) * q_decay)
        s = s * chunk_decay + jnp.einsum('bhjd,bhjv->bhdv', kc * k_decay, vc)
        return s, o

    s_fin, o = lax.scan(step, s0, (to_chunks(q), to_chunks(k), to_chunks(v)))
    o = o.transpose(1, 2, 0, 3, 4).reshape(B, H, L, v.shape[-1])
    return o, s_fin


def retention_mixer(h, w_in, w_out, decay_ls, s0, angs):
    B, L, _ = h.shape
    proj = h @ w_in
    q, k, v, g = jnp.split(proj, [RET_QK, 2 * RET_QK, 2 * RET_QK + RET_V], axis=-1)
    q = q.reshape(B, L, N_HEADS, HEAD_K).transpose(0, 2, 1, 3)
    k = k.reshape(B, L, N_HEADS, HEAD_K).transpose(0, 2, 1, 3) * (HEAD_K ** -0.5)
    v = v.reshape(B, L, N_HEADS, HEAD_V).transpose(0, 2, 1, 3)
    if angs is not None:
        q = apply_rope_2d(q, angs)
        k = apply_rope_2d(k, angs)
    q, k, v = q.astype(jnp.float32), k.astype(jnp.float32), v.astype(jnp.float32)
    if s0 is None:
        s0 = jnp.zeros((B, 2, N_HEADS, HEAD_K, HEAD_V), jnp.float32)
    s0 = s0.astype(jnp.float32)
    log_gamma = -jnp.exp(decay_ls.astype(jnp.float32))
    o_f, s_f = retention_scan(q, k, v, log_gamma[0], s0[:, 0])
    o_b, s_b = retention_scan(jnp.flip(q, 2), jnp.flip(k, 2), jnp.flip(v, 2), log_gamma[1], s0[:, 1])
    o = o_f + jnp.flip(o_b, 2)
    mu = jnp.mean(o, axis=-1, keepdims=True)
    var = jnp.mean(jnp.square(o - mu), axis=-1, keepdims=True)
    o = (o - mu) * lax.rsqrt(var + EPS)
    o = o.transpose(0, 2, 1, 3).reshape(B, L, RET_V).astype(h.dtype)
    out = (jax.nn.silu(g) * o) @ w_out
    return out, jnp.stack([s_f, s_b], axis=1)


def conv_ffn(h, w_up, conv_w, conv_b, w_down):
    u = h @ w_up
    up = jnp.pad(u, ((0, 0), (1, 1), (0, 0)))
    u = up[:, :-2] * conv_w[0] + up[:, 1:-1] * conv_w[1] + up[:, 2:] * conv_w[2] + conv_b
    a, gt = jnp.split(u, 2, axis=-1)
    return (jax.nn.silu(gt) * a) @ w_down


def trunk(x, cond, states_in, angs, mod_w, mod_b, norm_g, final_norm_g, pool_w, pool_scale,
          ret_w_in, ret_w_out, ret_decay_ls, ffn_w_up, ffn_conv_w, ffn_conv_b, ffn_w_down):
    states_out = []
    for l in range(DEPTH):
        mod = (jax.nn.silu(cond) @ mod_w[l] + mod_b[l])[:, None, :]
        sh1, sc1, g1, sh2, sc2, g2 = jnp.split(mod, 6, axis=-1)
        r = l // N_MIXERS
        h = rmsnorm(x, norm_g[l, 0]) * (1.0 + sc1) + sh1
        if l % N_MIXERS == 0:
            out = pool_mixer(h, pool_w[r], pool_scale[r])
        else:
            s0 = None if states_in is None else states_in[:, r]
            out, st = retention_mixer(h, ret_w_in[r], ret_w_out[r], ret_decay_ls[r], s0, angs)
            states_out.append(st)
        x = x + g1 * out
        h = rmsnorm(x, norm_g[l, 1]) * (1.0 + sc2) + sh2
        x = x + g2 * conv_ffn(h, ffn_w_up[l], ffn_conv_w[l], ffn_conv_b[l], ffn_w_down[l])
    return rmsnorm(x, final_norm_g), jnp.stack(states_out, axis=1)


def setup_inputs(seed: int = 0) -> dict:
    key = jax.random.key(seed)
    ks = jax.random.split(key, 20)
    f32 = jnp.float32
    nrm = lambda k, shape, s: jax.random.normal(k, shape, f32) * s
    base_ls = jnp.log(-jnp.log1p(-(2.0 ** (-5.0 - jnp.arange(N_HEADS, dtype=f32)))))
    return {
        "x_prompt": nrm(ks[0], (BATCH, SEQ, D_MODEL), 1.0),
        "x_sample": nrm(ks[1], (DEC_BATCH, DEC_SEQ, D_MODEL), 1.0),
        "state_ret": nrm(ks[2], (DEC_BATCH, N_RET_LAYERS, 2, N_HEADS, HEAD_K, HEAD_V), 0.1),
        "c": nrm(ks[3], (DEC_BATCH, D_MODEL), 1.0),
        "c_ctx": nrm(ks[4], (D_MODEL,), 1.0),
        "mod_w": nrm(ks[5], (DEPTH, D_MODEL, 6 * D_MODEL), 0.2 * D_MODEL ** -0.5),
        "mod_b": nrm(ks[6], (DEPTH, 6 * D_MODEL), 0.02),
        "norm_g": 1.0 + nrm(ks[7], (DEPTH, 2, D_MODEL), 0.02),
        "final_norm_g": 1.0 + nrm(ks[8], (D_MODEL,), 0.02),
        "pool_w": nrm(ks[9], (N_POOL_LAYERS, N_POOL_GROUPS, POOL_GROUP, POOL_GROUP), POOL_GROUP ** -0.5),
        "pool_scale": 1.0 + nrm(ks[10], (N_POOL_LAYERS, D_MODEL), 0.02),
        "ret_w_in": nrm(ks[11], (N_RET_LAYERS, D_MODEL, RET_IN), D_MODEL ** -0.5),
        "ret_w_out": nrm(ks[12], (N_RET_LAYERS, RET_V, D_MODEL), RET_V ** -0.5),
        "ret_decay_ls": base_ls + nrm(ks[13], (N_RET_LAYERS, 2, N_HEADS), 0.05),
        "ffn_w_up": nrm(ks[14], (DEPTH, D_MODEL, 2 * D_FF), D_MODEL ** -0.5),
        "ffn_conv_w": nrm(ks[15], (DEPTH, CONV_W, 2 * D_FF), CONV_W ** -0.5),
        "ffn_conv_b": nrm(ks[16], (DEPTH, 2 * D_FF), 0.02),
        "ffn_w_down": nrm(ks[17], (DEPTH, D_FF, D_MODEL), D_FF ** -0.5),
    }


def reference(x_prompt, x_sample, state_ret, c, c_ctx, mod_w, mod_b, norm_g, final_norm_g,
              pool_w, pool_scale, ret_w_in, ret_w_out, ret_decay_ls, ffn_w_up, ffn_conv_w,
              ffn_conv_b, ffn_w_down):
    y_prompt, new_state_ret = trunk(
        x_prompt, c_ctx[None, :], None, None, mod_w, mod_b, norm_g, final_norm_g, pool_w,
        pool_scale, ret_w_in, ret_w_out, ret_decay_ls, ffn_w_up, ffn_conv_w, ffn_conv_b, ffn_w_down)
    angs = rope_2d_angles(x_sample.shape[1])
    y_sample, _ = trunk(
        x_sample, c, state_ret, angs, mod_w, mod_b, norm_g, final_norm_g, pool_w,
        pool_scale, ret_w_in, ret_w_out, ret_decay_ls, ffn_w_up, ffn_conv_w, ffn_conv_b, ffn_w_down)
    return (y_prompt, y_sample, new_state_ret)
```

```python
import functools
from typing import NamedTuple

import jax
import jax.numpy as jnp
from jax import lax
from jax.experimental import pallas as pl
from jax.experimental.pallas import tpu as pltpu

F32 = jnp.float32
BF16 = jnp.bfloat16

EPS = 1e-6
CHUNK = 128
GRID_W = 64
ROPE_BASE = 10000.0
POOL_WINDOWS = (2, 4, 8, 16)
N_MIXERS = 2

V7X_LANES = 128
V7X_VMEM_BYTES = 64 * 1024 * 1024
VMEM_LIMIT = V7X_VMEM_BYTES - 8 * 1024 * 1024
N_COND_ROWS = 16
ROW_TILE = 256
MM_TILE = 1024
FF_ALIGN = 1024
FF_UP_TN = 512
N_K_STEPS = 4


class Geom(NamedTuple):
    tp: int
    ts: int
    s: int
    ds: int
    d: int
    bseq: int

    @property
    def t(self):
        return self.tp + self.ts


def _cond_index(g: Geom, i, bm):
    return jnp.maximum(i * bm - g.tp + g.ds, 0) // g.ds


def _mod_spec(g: Geom, l, which, bm, bn=None, col_axis=None):
    bn = g.d if bn is None else bn

    def index(*ids):
        j = 0 if col_axis is None else ids[col_axis]
        return (l, which, _cond_index(g, ids[0], bm), 0, j)

    return pl.BlockSpec((None, None, None, 1, bn), index)


def _params(n_axes):
    return pltpu.CompilerParams(dimension_semantics=("arbitrary",) * n_axes,
                                vmem_limit_bytes=VMEM_LIMIT)


def _pick_tile(n, cap, align=V7X_LANES):
    best = None
    for t in range(align, min(n, cap) + 1, align):
        if n % t == 0:
            best = t
    assert best is not None, (n, cap)
    return best


def _rem_static(x, n):
    return x & (n - 1) if n & (n - 1) == 0 else lax.rem(x, n)


def _silu(x):
    return x * jax.nn.sigmoid(x)


def _mod_kernel(c_ref, w_ref, b_ref, o_ref):
    a = _silu(c_ref[...]).astype(BF16)
    acc = jnp.dot(a, w_ref[...].astype(BF16), preferred_element_type=F32)
    o_ref[...] = acc + b_ref[...]


def _modulation(cond, mod_w, mod_b):
    depth, d, n6 = mod_w.shape
    tn = _pick_tile(n6, 512)
    out = pl.pallas_call(
        _mod_kernel,
        grid=(depth, n6 // tn),
        in_specs=[
            pl.BlockSpec((N_COND_ROWS, d), lambda l, j: (0, 0)),
            pl.BlockSpec((None, d, tn), lambda l, j: (l, 0, j)),
            pl.BlockSpec((None, 1, tn), lambda l, j: (l, 0, j)),
        ],
        out_specs=pl.BlockSpec((None, N_COND_ROWS, tn), lambda l, j: (l, 0, j)),
        out_shape=jax.ShapeDtypeStruct((depth, N_COND_ROWS, n6), F32),
        compiler_params=_params(2),
        name="adaln_modulation",
    )(cond, mod_w, mod_b.reshape(depth, 1, n6))
    out = out.reshape(depth, N_COND_ROWS, 6, 1, d)
    return jnp.transpose(out, (0, 2, 1, 3, 4))


def _norm_mod_kernel(x_ref, g_ref, sc_ref, sh_ref, o_ref, *, j):
    x = x_ref[...]
    y = x * lax.rsqrt(jnp.mean(x * x, axis=-1, keepdims=True) + EPS)
    y = y * g_ref[j:j + 1, :]
    o_ref[...] = (y * (1.0 + sc_ref[...]) + sh_ref[...]).astype(o_ref.dtype)


def _norm_mod(g: Geom, x, norm_g, mod, l, j, out_dtype):
    bm = ROW_TILE
    return pl.pallas_call(
        functools.partial(_norm_mod_kernel, j=j),
        grid=(g.t // bm,),
        in_specs=[
            pl.BlockSpec((bm, g.d), lambda i: (i, 0)),
            pl.BlockSpec((None, 2, g.d), lambda i: (l, 0, 0)),
            _mod_spec(g, l, 3 * j + 1, bm),
            _mod_spec(g, l, 3 * j, bm),
        ],
        out_specs=pl.BlockSpec((bm, g.d), lambda i: (i, 0)),
        out_shape=jax.ShapeDtypeStruct((g.t, g.d), out_dtype),
        compiler_params=_params(1),
        name=f"norm_mod_l{l}_{j}",
    )(x, norm_g, mod, mod)


def _final_norm_kernel(x_ref, g_ref, o_ref):
    x = x_ref[...]
    y = x * lax.rsqrt(jnp.mean(x * x, axis=-1, keepdims=True) + EPS)
    o_ref[...] = y * g_ref[...]


def _final_norm(g: Geom, x, final_g, row0, rows):
    bm = ROW_TILE
    blk0 = row0 // bm
    return pl.pallas_call(
        _final_norm_kernel,
        grid=(rows // bm,),
        in_specs=[
            pl.BlockSpec((bm, g.d), lambda i: (blk0 + i, 0)),
            pl.BlockSpec((1, g.d), lambda i: (0, 0)),
        ],
        out_specs=pl.BlockSpec((bm, g.d), lambda i: (i, 0)),
        out_shape=jax.ShapeDtypeStruct((rows, g.d), F32),
        compiler_params=_params(1),
        name=f"final_norm_{row0}",
    )(x, final_g.reshape(1, g.d))


def _mm_plain_kernel(a_ref, b_ref, o_ref):
    o_ref[...] = jnp.dot(a_ref[...], b_ref[...], preferred_element_type=F32).astype(o_ref.dtype)


def _matmul(a, w, r, out_dtype, name):
    t, k = a.shape
    n = w.shape[-1]
    bm = _pick_tile(t, MM_TILE)
    bn = _pick_tile(n, MM_TILE)
    return pl.pallas_call(
        _mm_plain_kernel,
        grid=(t // bm, n // bn),
        in_specs=[
            pl.BlockSpec((bm, k), lambda i, j: (i, 0)),
            pl.BlockSpec((None, k, bn), lambda i, j: (r, 0, j)),
        ],
        out_specs=pl.BlockSpec((bm, bn), lambda i, j: (i, j)),
        out_shape=jax.ShapeDtypeStruct((t, n), out_dtype),
        compiler_params=_params(2),
        name=name,
    )(a, w)


def _mm_residual_kernel(a_ref, b_ref, x_ref, g_ref, o_ref, acc_ref, *, nk):
    k = pl.program_id(2)
    part = jnp.dot(a_ref[...], b_ref[...], preferred_element_type=F32)

    @pl.when(k == 0)
    def _():
        acc_ref[...] = part

    @pl.when((k > 0) & (k < nk - 1))
    def _():
        acc_ref[...] += part

    @pl.when(k == nk - 1)
    def _():
        o_ref[...] = x_ref[...] + g_ref[...] * (acc_ref[...] + part)


def _matmul_residual(g: Geom, a, w, r, x, mod, l, gate, name):
    t, k = a.shape
    n = w.shape[-1]
    bm = _pick_tile(g.bseq, MM_TILE)
    bn = _pick_tile(n, MM_TILE)
    nk = N_K_STEPS
    bk = k // nk
    assert k % nk == 0 and bk % V7X_LANES == 0, (k, nk)
    return pl.pallas_call(
        functools.partial(_mm_residual_kernel, nk=nk),
        grid=(t // bm, n // bn, nk),
        in_specs=[
            pl.BlockSpec((bm, bk), lambda i, j, kk: (i, kk)),
            pl.BlockSpec((None, bk, bn), lambda i, j, kk: (r, kk, j)),
            pl.BlockSpec((bm, bn), lambda i, j, kk: (i, j)),
            _mod_spec(g, l, gate, bm, bn, col_axis=1),
        ],
        out_specs=pl.BlockSpec((bm, bn), lambda i, j, kk: (i, j)),
        out_shape=jax.ShapeDtypeStruct((t, n), F32),
        scratch_shapes=[pltpu.VMEM((bm, bn), F32)],
        compiler_params=_params(3),
        name=name,
    )(a, w, x, mod)


def _seq_position(g: Geom):
    is_prompt = pl.program_id(0) < g.tp // g.bseq
    t = lax.broadcasted_iota(jnp.int32, (g.bseq, 1), 0)
    pos = jnp.where(is_prompt, _rem_static(t, g.s), _rem_static(t, g.ds))
    length = jnp.where(is_prompt, g.s, g.ds)
    return pos, length


def _ffn_up_kernel(h_ref, wa_ref, wg_ref, cwa_ref, cwg_ref, cba_ref, cbg_ref, o_ref, *, g: Geom):
    bm = h_ref.shape[0]
    h = h_ref[...]
    pos, length = _seq_position(g)
    first = pos == 0
    last = pos == length - 1

    def conv(w_ref, cw_ref, cb_ref):
        u = jnp.dot(h, w_ref[...], preferred_element_type=F32)
        prev = jnp.where(first, 0.0, pltpu.roll(u, 1, axis=0))
        nxt = jnp.where(last, 0.0, pltpu.roll(u, bm - 1, axis=0))
        return prev * cw_ref[0:1, :] + u * cw_ref[1:2, :] + nxt * cw_ref[2:3, :] + cb_ref[...]

    a = conv(wa_ref, cwa_ref, cba_ref)
    gt = conv(wg_ref, cwg_ref, cbg_ref)
    o_ref[...] = (_silu(gt) * a).astype(o_ref.dtype)


def _ffn_up(g: Geom, h, w_up, conv_w, conv_b, l):
    nfp = w_up.shape[-1] // 2
    tn = _pick_tile(nfp, FF_UP_TN)
    nfb = nfp // tn
    bm = g.bseq
    return pl.pallas_call(
        functools.partial(_ffn_up_kernel, g=g),
        grid=(g.t // bm, nfb),
        in_specs=[
            pl.BlockSpec((bm, g.d), lambda i, j: (i, 0)),
            pl.BlockSpec((None, g.d, tn), lambda i, j: (l, 0, j)),
            pl.BlockSpec((None, g.d, tn), lambda i, j: (l, 0, nfb + j)),
            pl.BlockSpec((None, 3, tn), lambda i, j: (l, 0, j)),
            pl.BlockSpec((None, 3, tn), lambda i, j: (l, 0, nfb + j)),
            pl.BlockSpec((None, 1, tn), lambda i, j: (l, 0, j)),
            pl.BlockSpec((None, 1, tn), lambda i, j: (l, 0, nfb + j)),
        ],
        out_specs=pl.BlockSpec((bm, tn), lambda i, j: (i, j)),
        out_shape=jax.ShapeDtypeStruct((g.t, nfp), BF16),
        compiler_params=_params(2),
        name=f"ffn_up_l{l}",
    )(h, w_up, w_up, conv_w, conv_w, conv_b, conv_b)


def _pool_kernel(h_ref, w_ref, ps_ref, x_ref, g_ref, o_ref, *, g: Geom):
    bm = h_ref.shape[0]
    gi = pl.program_id(1)
    half = jnp.int32(0)
    for k, win in enumerate(POOL_WINDOWS):
        half = jnp.where(gi == k, win // 2, half)

    pos, length = _seq_position(g)
    off = lax.broadcasted_iota(jnp.int32, (bm, bm), 1) - lax.broadcasted_iota(jnp.int32, (bm, bm), 0)
    src = pos + off
    band = (off >= -half) & (off < half) & (src >= 0) & (src < length)
    band = jnp.where(band, 1.0, 0.0).astype(BF16)
    cnt = (jnp.minimum(pos + half, length) - jnp.maximum(pos - half, 0)).astype(F32)

    h = h_ref[...]
    h_hi = h.astype(BF16)
    h_lo = (h - h_hi.astype(F32)).astype(BF16)
    wsum = (jnp.dot(band, h_hi, preferred_element_type=F32)
            + jnp.dot(band, h_lo, preferred_element_type=F32))
    pooled = (wsum / cnt - h).astype(BF16)
    y = jnp.dot(pooled, w_ref[...], preferred_element_type=F32) * ps_ref[...]
    o_ref[...] = x_ref[...] + g_ref[...] * y


def _pool_mixer(g: Geom, h, pool_w, pool_scale, x, mod, l, r):
    ng, pg = pool_w.shape[1], pool_w.shape[2]
    assert ng == len(POOL_WINDOWS) and pg % V7X_LANES == 0
    bm = g.bseq
    return pl.pallas_call(
        functools.partial(_pool_kernel, g=g),
        grid=(g.t // bm, ng),
        in_specs=[
            pl.BlockSpec((bm, pg), lambda i, gi: (i, gi)),
            pl.BlockSpec((None, None, pg, pg), lambda i, gi: (r, gi, 0, 0)),
            pl.BlockSpec((None, 1, pg), lambda i, gi: (r, 0, gi)),
            pl.BlockSpec((bm, pg), lambda i, gi: (i, gi)),
            _mod_spec(g, l, 2, bm, pg, col_axis=1),
        ],
        out_specs=pl.BlockSpec((bm, pg), lambda i, gi: (i, gi)),
        out_shape=jax.ShapeDtypeStruct((g.t, g.d), F32),
        compiler_params=_params(2),
        name=f"pool_mixer_l{l}",
    )(h, pool_w, pool_scale.reshape(pool_scale.shape[0], 1, g.d), x, mod)


def _ret_kernel(*refs, r, n_heads, seq, rope, has_s0, emit_state, n_aliased):
    it = iter(refs)
    dls_ref = next(it)
    q_ref, k_ref, v_ref, gate_ref = next(it), next(it), next(it), next(it)
    cos_ref, sin_ref = (next(it), next(it)) if rope else (None, None)
    s0_ref = next(it) if has_s0 else None
    for _ in range(n_aliased):
        next(it)
    o_ref = next(it)
    st_ref = next(it) if emit_state else None
    qr_ref, kr_ref, oacc_ref, sf_ref, sb_ref = next(it), next(it), next(it), next(it), next(it)

    hk = q_ref.shape[1]
    head = pl.program_id(1)
    n_chunks = seq // CHUNK

    q = q_ref[...].astype(F32)
    k = k_ref[...].astype(F32) * (hk ** -0.5)
    if rope:
        hh = hk // 2

        def rot(x):
            parts = []
            for p in range(2):
                xp = x[:, p * hh:(p + 1) * hh]
                parts.append(xp * cos_ref[:, p * hh:(p + 1) * hh]
                             + pltpu.roll(xp, hh // 2, axis=1) * sin_ref[:, p * hh:(p + 1) * hh])
            return jnp.concatenate(parts, axis=1)

        q, k = rot(q), rot(k)
    qr_ref[...] = q.astype(BF16)
    kr_ref[...] = k

    def log_gamma(direction):
        ls = dls_ref[(r * 2 + direction) * n_heads + head]
        return -jnp.exp(jnp.full((1, 1), ls, F32))

    lg_f, lg_b = log_gamma(0), log_gamma(1)
    ri = lax.broadcasted_iota(jnp.int32, (CHUNK, CHUNK), 0).astype(F32)
    ci = lax.broadcasted_iota(jnp.int32, (CHUNK, CHUNK), 1).astype(F32)
    diff = ri - ci
    intra = (jnp.where(diff >= 0, jnp.exp(lg_f * jnp.maximum(diff, 0.0)), 0.0)
             + jnp.where(diff <= 0, jnp.exp(lg_b * jnp.maximum(-diff, 0.0)), 0.0))
    idx = lax.broadcasted_iota(jnp.int32, (CHUNK, 1), 0).astype(F32)
    qdec_f = jnp.exp(lg_f * (idx + 1.0))
    kdec_f = jnp.exp(lg_f * (CHUNK - 1.0 - idx))
    qdec_b = jnp.exp(lg_b * (CHUNK - idx))
    kdec_b = jnp.exp(lg_b * idx)
    cdec_f = jnp.exp(lg_f * CHUNK)
    cdec_b = jnp.exp(lg_b * CHUNK)

    if has_s0:
        sf_ref[...] = s0_ref[0]
        sb_ref[...] = s0_ref[1]
    else:
        sf_ref[...] = jnp.zeros_like(sf_ref)
        sb_ref[...] = jnp.zeros_like(sb_ref)

    def chunk(c):
        rows = pl.ds(pl.multiple_of(c * CHUNK, CHUNK), CHUNK)
        return rows, qr_ref[rows, :], kr_ref[rows, :], v_ref[rows, :]

    def state_update(s_ref, s, kf, kdec, cdec, vc):
        kd = (kf * kdec).astype(BF16)
        s_ref[...] = s * cdec + lax.dot_general(kd, vc, (((0,), (0,)), ((), ())),
                                                preferred_element_type=F32)

    def forward(c, carry):
        rows, qc, kf, vc = chunk(c)
        scores = lax.dot_general(qc, kf.astype(BF16), (((1,), (1,)), ((), ())),
                                 preferred_element_type=F32)
        o = jnp.dot((scores * intra).astype(BF16), vc, preferred_element_type=F32)
        s = sf_ref[...]
        o = o + jnp.dot(qc, s.astype(BF16), preferred_element_type=F32) * qdec_f
        oacc_ref[rows, :] = o
        state_update(sf_ref, s, kf, kdec_f, cdec_f, vc)
        return carry

    def backward(m, carry):
        rows, qc, kf, vc = chunk(n_chunks - 1 - m)
        s = sb_ref[...]
        o = oacc_ref[rows, :] + jnp.dot(qc, s.astype(BF16), preferred_element_type=F32) * qdec_b
        mu = jnp.mean(o, axis=-1, keepdims=True)
        dev = o - mu
        var = jnp.mean(dev * dev, axis=-1, keepdims=True)
        normed = dev * lax.rsqrt(var + EPS)
        gate = gate_ref[rows, :].astype(F32)
        o_ref[rows, :] = (_silu(gate) * normed).astype(o_ref.dtype)
        state_update(sb_ref, s, kf, kdec_b, cdec_b, vc)
        return carry

    lax.fori_loop(0, n_chunks, forward, 0)
    lax.fori_loop(0, n_chunks, backward, 0)

    if emit_state:
        st_ref[0] = sf_ref[...]
        st_ref[1] = sb_ref[...]


def _retention(proj, decay_ls, r, n_heads, row0, n_seq, seq, o_buf=None, rope_tables=None, s0=None,
               state_shape=None, state_buf=None):
    t, ret_in = proj.shape
    hk = ret_in // (6 * n_heads)
    hv = 2 * hk
    assert seq % CHUNK == 0 and row0 % seq == 0
    rb0 = row0 // seq
    rope = rope_tables is not None
    has_s0 = s0 is not None
    emit_state = state_shape is not None

    in_specs = [
        pl.BlockSpec((seq, hk), lambda b, h, *_: (rb0 + b, h)),
        pl.BlockSpec((seq, hk), lambda b, h, *_: (rb0 + b, n_heads + h)),
        pl.BlockSpec((seq, hv), lambda b, h, *_: (rb0 + b, n_heads + h)),
        pl.BlockSpec((seq, hv), lambda b, h, *_: (rb0 + b, 2 * n_heads + h)),
    ]
    args = [proj, proj, proj, proj]
    if rope:
        in_specs += [pl.BlockSpec((seq, hk), lambda b, h, *_: (0, 0))] * 2
        args += list(rope_tables)
    if has_s0:
        in_specs.append(pl.BlockSpec((None, None, 2, None, hk, hv), lambda b, h, *_: (b, r, 0, h, 0, 0)))
        args.append(s0)
    aliases = {}
    out_specs = [pl.BlockSpec((seq, hv), lambda b, h, *_: (rb0 + b, h))]
    out_shape = [jax.ShapeDtypeStruct((t, n_heads * hv), BF16)]
    if emit_state:
        out_specs.append(pl.BlockSpec((None, None, 2, None, hk, hv), lambda b, h, *_: (b, r, 0, h, 0, 0)))
        out_shape.append(jax.ShapeDtypeStruct(state_shape, F32))
    for out_idx, buf in enumerate((o_buf, state_buf)):
        if buf is not None:
            aliases[1 + len(args)] = out_idx
            in_specs.append(pl.BlockSpec(memory_space=pl.ANY))
            args.append(buf)

    out = pl.pallas_call(
        functools.partial(_ret_kernel, r=r, n_heads=n_heads, seq=seq, rope=rope, has_s0=has_s0,
                          emit_state=emit_state, n_aliased=len(aliases)),
        grid_spec=pltpu.PrefetchScalarGridSpec(
            num_scalar_prefetch=1,
            grid=(n_seq, n_heads),
            in_specs=in_specs,
            out_specs=out_specs,
            scratch_shapes=[
                pltpu.VMEM((seq, hk), BF16),
                pltpu.VMEM((seq, hk), F32),
                pltpu.VMEM((seq, hv), F32),
                pltpu.VMEM((hk, hv), F32),
                pltpu.VMEM((hk, hv), F32),
            ],
        ),
        out_shape=out_shape,
        input_output_aliases=aliases,
        compiler_params=_params(2),
        name=f"retention_r{r}_{'latent' if rope else 'prompt'}",
    )(decay_ls.reshape(-1), *args)
    return out if emit_state else (out[0], None)


def _rope_tables(seq, hk):
    rows = seq // GRID_W
    row_ids = jnp.repeat(jnp.arange(rows), GRID_W).astype(F32)
    col_ids = jnp.tile(jnp.arange(GRID_W), rows).astype(F32)
    n = hk // 4
    freqs = ROPE_BASE ** (-jnp.arange(n, dtype=F32) / n)
    cos, sin = [], []
    for ang in (row_ids[:, None] * freqs, col_ids[:, None] * freqs):
        c, s = jnp.cos(ang), jnp.sin(ang)
        cos += [c, c]
        sin += [-s, s]
    return jnp.concatenate(cos, axis=1), jnp.concatenate(sin, axis=1)


def _pad_ff(a, nf, nfp, axis):
    pad = [(0, 0)] * a.ndim
    pad[axis] = (0, nfp - nf)
    return jnp.pad(a, pad)


def _pad_gated(a, nf, nfp):
    return jnp.concatenate([_pad_ff(a[..., :nf], nf, nfp, -1), _pad_ff(a[..., nf:], nf, nfp, -1)], axis=-1)


def kernel(x_prompt, x_sample, state_ret, c, c_ctx, mod_w, mod_b, norm_g, final_norm_g, pool_w, pool_scale,
           ret_w_in, ret_w_out, ret_decay_ls, ffn_w_up, ffn_conv_w, ffn_conv_b, ffn_w_down):
    b, s, d = x_prompt.shape
    db, ds, _ = x_sample.shape
    depth = mod_w.shape[0]
    n_heads = ret_decay_ls.shape[-1]
    n_ret = ret_w_in.shape[0]
    hk = d // n_heads
    hv = ret_w_out.shape[1] // n_heads
    nf = ffn_w_down.shape[1]
    nfp = -(-nf // FF_ALIGN) * FF_ALIGN

    bseq = max(s, ds)
    g = Geom(tp=b * s, ts=db * ds, s=s, ds=ds, d=d, bseq=bseq)
    assert bseq % s == 0 and bseq % ds == 0 and g.tp % bseq == 0 and g.ts % bseq == 0
    assert g.tp % ds == 0 and s % ROW_TILE == 0 and ds % ROW_TILE == 0
    assert 1 + db <= N_COND_ROWS and hv == 2 * hk and ds % GRID_W == 0

    pool_w_b = pool_w.astype(BF16)
    w_in_b = ret_w_in.astype(BF16)
    w_out_b = ret_w_out.astype(BF16)
    w_up_b = _pad_gated(ffn_w_up.astype(BF16), nf, nfp)
    w_down_b = _pad_ff(ffn_w_down.astype(BF16), nf, nfp, 1)
    conv_w = _pad_gated(ffn_conv_w, nf, nfp)
    conv_b = _pad_gated(ffn_conv_b, nf, nfp).reshape(depth, 1, 2 * nfp)

    cond = jnp.concatenate([c_ctx[None, :], c, jnp.zeros((N_COND_ROWS - 1 - db, d), F32)], axis=0)
    mod = _modulation(cond, mod_w, mod_b)

    x = jnp.concatenate([x_prompt.reshape(g.tp, d), x_sample.reshape(g.ts, d)], axis=0)
    rope_tables = _rope_tables(ds, hk)
    state_shape = (b, n_ret, 2, n_heads, hk, hv)
    state_buf = None

    for l in range(depth):
        r = l // N_MIXERS
        if l % N_MIXERS == 0:
            h = _norm_mod(g, x, norm_g, mod, l, 0, F32)
            x = _pool_mixer(g, h, pool_w_b, pool_scale, x, mod, l, r)
        else:
            h = _norm_mod(g, x, norm_g, mod, l, 0, BF16)
            proj = _matmul(h, w_in_b, r, BF16, f"ret_in_l{l}")
            o_buf, state_buf = _retention(proj, ret_decay_ls, r, n_heads, 0, b, s,
                                          state_shape=state_shape, state_buf=state_buf)
            o_buf, _ = _retention(proj, ret_decay_ls, r, n_heads, g.tp, db, ds, o_buf=o_buf,
                                  rope_tables=rope_tables, s0=state_ret)
            x = _matmul_residual(g, o_buf, w_out_b, r, x, mod, l, 2, f"ret_out_l{l}")
        h = _norm_mod(g, x, norm_g, mod, l, 1, BF16)
        act = _ffn_up(g, h, w_up_b, conv_w, conv_b, l)
        x = _matmul_residual(g, act, w_down_b, l, x, mod, l, 5, f"ffn_down_l{l}")

    y_prompt = _final_norm(g, x, final_norm_g, 0, g.tp).reshape(b, s, d)
    y_sample = _final_norm(g, x, final_norm_g, g.tp, g.ts).reshape(db, ds, d)
    return y_prompt, y_sample, state_buf
```

```python
import functools
from typing import NamedTuple

import jax
import jax.numpy as jnp
from jax import lax
from jax.experimental import pallas as pl
from jax.experimental.pallas import tpu as pltpu

F32 = jnp.float32
BF16 = jnp.bfloat16

EPS = 1e-6
CHUNK = 128
GRID_W = 64
ROPE_BASE = 10000.0
POOL_WINDOWS = (2, 4, 8, 16)
N_MIXERS = 2

V7X_LANES = 128
V7X_VMEM_BYTES = 64 * 1024 * 1024
VMEM_LIMIT = V7X_VMEM_BYTES - 8 * 1024 * 1024
N_COND_ROWS = 16
ROW_TILE = 256
MM_TILE = 1024
FF_ALIGN = 1024
FF_UP_TN = 256
FF_UP_BM = 2048
N_K_STEPS = 4
RET_HEADS_PER_STEP = 2


class Geom(NamedTuple):
    tp: int
    ts: int
    s: int
    ds: int
    d: int
    bseq: int

    @property
    def t(self):
        return self.tp + self.ts


def _cond_index(g: Geom, i, bm):
    return jnp.maximum(i * bm - g.tp + g.ds, 0) // g.ds


def _mod_spec(g: Geom, l, which, bm, bn=None, col_axis=None):
    bn = g.d if bn is None else bn

    def index(*ids):
        j = 0 if col_axis is None else ids[col_axis]
        return (l, which, _cond_index(g, ids[0], bm), 0, j)

    return pl.BlockSpec((None, None, None, 1, bn), index)


def _params(n_axes, vmem_limit=VMEM_LIMIT):
    return pltpu.CompilerParams(dimension_semantics=("arbitrary",) * n_axes,
                                vmem_limit_bytes=vmem_limit)


def _pick_tile(n, cap, align=V7X_LANES):
    best = None
    for t in range(align, min(n, cap) + 1, align):
        if n % t == 0:
            best = t
    assert best is not None, (n, cap)
    return best


def _rem_static(x, n):
    return x & (n - 1) if n & (n - 1) == 0 else lax.rem(x, n)


def _silu(x):
    return x * jax.nn.sigmoid(x)


def _mod_kernel(c_ref, w_ref, b_ref, o_ref):
    a = _silu(c_ref[...]).astype(BF16)
    acc = jnp.dot(a, w_ref[...].astype(BF16), preferred_element_type=F32)
    o_ref[...] = acc + b_ref[...]


def _modulation(cond, mod_w, mod_b):
    depth, d, n6 = mod_w.shape
    tn = _pick_tile(n6, 512)
    out = pl.pallas_call(
        _mod_kernel,
        grid=(depth, n6 // tn),
        in_specs=[
            pl.BlockSpec((N_COND_ROWS, d), lambda l, j: (0, 0)),
            pl.BlockSpec((None, d, tn), lambda l, j: (l, 0, j)),
            pl.BlockSpec((None, 1, tn), lambda l, j: (l, 0, j)),
        ],
        out_specs=pl.BlockSpec((None, N_COND_ROWS, tn), lambda l, j: (l, 0, j)),
        out_shape=jax.ShapeDtypeStruct((depth, N_COND_ROWS, n6), F32),
        compiler_params=_params(2),
        name="adaln_modulation",
    )(cond, mod_w, mod_b.reshape(depth, 1, n6))
    out = out.reshape(depth, N_COND_ROWS, 6, 1, d)
    return jnp.transpose(out, (0, 2, 1, 3, 4))


def _norm_mod_kernel(x_ref, g_ref, sc_ref, sh_ref, o_ref, *, j):
    x = x_ref[...]
    y = x * lax.rsqrt(jnp.mean(x * x, axis=-1, keepdims=True) + EPS)
    y = y * g_ref[j:j + 1, :]
    o_ref[...] = (y * (1.0 + sc_ref[...]) + sh_ref[...]).astype(o_ref.dtype)


def _norm_mod(g: Geom, x, norm_g, mod, l, j, out_dtype):
    bm = ROW_TILE
    return pl.pallas_call(
        functools.partial(_norm_mod_kernel, j=j),
        grid=(g.t // bm,),
        in_specs=[
            pl.BlockSpec((bm, g.d), lambda i: (i, 0)),
            pl.BlockSpec((None, 2, g.d), lambda i: (l, 0, 0)),
            _mod_spec(g, l, 3 * j + 1, bm),
            _mod_spec(g, l, 3 * j, bm),
        ],
        out_specs=pl.BlockSpec((bm, g.d), lambda i: (i, 0)),
        out_shape=jax.ShapeDtypeStruct((g.t, g.d), out_dtype),
        compiler_params=_params(1),
        name=f"norm_mod_l{l}_{j}",
    )(x, norm_g, mod, mod)


def _final_norm_kernel(x_ref, g_ref, o_ref):
    x = x_ref[...]
    y = x * lax.rsqrt(jnp.mean(x * x, axis=-1, keepdims=True) + EPS)
    o_ref[...] = y * g_ref[...]


def _final_norm(g: Geom, x, final_g, row0, rows):
    bm = ROW_TILE
    blk0 = row0 // bm
    return pl.pallas_call(
        _final_norm_kernel,
        grid=(rows // bm,),
        in_specs=[
            pl.BlockSpec((bm, g.d), lambda i: (blk0 + i, 0)),
            pl.BlockSpec((1, g.d), lambda i: (0, 0)),
        ],
        out_specs=pl.BlockSpec((bm, g.d), lambda i: (i, 0)),
        out_shape=jax.ShapeDtypeStruct((rows, g.d), F32),
        compiler_params=_params(1),
        name=f"final_norm_{row0}",
    )(x, final_g.reshape(1, g.d))


def _mm_plain_kernel(a_ref, b_ref, o_ref):
    b = b_ref[...].astype(BF16)
    o_ref[...] = jnp.dot(a_ref[...], b, preferred_element_type=F32).astype(o_ref.dtype)


def _matmul(a, w, r, out_dtype, name):
    t, k = a.shape
    n = w.shape[-1]
    bm = _pick_tile(t, 2 * MM_TILE)
    bn = _pick_tile(n, MM_TILE // 2)
    return pl.pallas_call(
        _mm_plain_kernel,
        grid=(t // bm, n // bn),
        in_specs=[
            pl.BlockSpec((bm, k), lambda i, j: (i, 0), pipeline_mode=pl.Buffered(1)),
            pl.BlockSpec((None, k, bn), lambda i, j: (r, 0, j)),
        ],
        out_specs=pl.BlockSpec((bm, bn), lambda i, j: (i, j)),
        out_shape=jax.ShapeDtypeStruct((t, n), out_dtype),
        compiler_params=_params(2),
        name=name,
    )(a, w)


def _mm_residual_kernel(a_ref, b_ref, x_ref, g_ref, o_ref, acc_ref, *, nk):
    k = pl.program_id(2)

    @pl.when((pl.program_id(0) == 0) & (pl.program_id(1) == 0) & (k == 0))
    def _():
        acc_ref[...] = jnp.zeros_like(acc_ref)

    acc = acc_ref[...] + jnp.dot(a_ref[...], b_ref[...], preferred_element_type=F32)
    acc_ref[...] = jnp.where(k == nk - 1, 0.0, acc)
    o_ref[...] = x_ref[...] + g_ref[...] * acc


def _matmul_residual(g: Geom, a, w, r, x, mod, l, gate, name):
    t, k = a.shape
    n = w.shape[-1]
    bm = _pick_tile(g.bseq, MM_TILE)
    bn = _pick_tile(n, MM_TILE)
    nk = N_K_STEPS
    bk = k // nk
    assert k % nk == 0 and bk % V7X_LANES == 0, (k, nk)
    return pl.pallas_call(
        functools.partial(_mm_residual_kernel, nk=nk),
        grid=(t // bm, n // bn, nk),
        in_specs=[
            pl.BlockSpec((bm, bk), lambda i, j, kk: (i, kk)),
            pl.BlockSpec((None, bk, bn), lambda i, j, kk: (r, kk, j)),
            pl.BlockSpec((bm, bn), lambda i, j, kk: (i, j)),
            _mod_spec(g, l, gate, bm, bn, col_axis=1),
        ],
        out_specs=pl.BlockSpec((bm, bn), lambda i, j, kk: (i, j)),
        out_shape=jax.ShapeDtypeStruct((t, n), F32),
        scratch_shapes=[pltpu.VMEM((bm, bn), F32)],
        compiler_params=_params(3),
        name=name,
    )(a, w, x, mod)


def _seq_position(g: Geom, bm):
    is_prompt = pl.program_id(0) < g.tp // bm
    t = lax.broadcasted_iota(jnp.int32, (g.bseq, 1), 0)
    pos = jnp.where(is_prompt, _rem_static(t, g.s), _rem_static(t, g.ds))
    length = jnp.where(is_prompt, g.s, g.ds)
    return pos, length


def _ffn_up_kernel(h_ref, wa_ref, wg_ref, cwa_ref, cwg_ref, cba_ref, cbg_ref, o_ref, *, g: Geom, n_real):
    bm = h_ref.shape[0]
    j = pl.program_id(1)

    @pl.when(j >= n_real)
    def _():
        o_ref[...] = jnp.zeros_like(o_ref)

    @pl.when(j < n_real)
    def _():
        pos, length = _seq_position(g, bm)
        first = pos == 0
        last = pos == length - 1
        wa = wa_ref[...].astype(BF16)
        wg = wg_ref[...].astype(BF16)

        for r0 in range(0, bm, g.bseq):
            h = h_ref[r0:r0 + g.bseq, :]

            def conv(w, cw_ref, cb_ref):
                u = jnp.dot(h, w, preferred_element_type=F32)
                prev = jnp.where(first, 0.0, pltpu.roll(u, 1, axis=0))
                nxt = jnp.where(last, 0.0, pltpu.roll(u, g.bseq - 1, axis=0))
                return prev * cw_ref[0:1, :] + u * cw_ref[1:2, :] + nxt * cw_ref[2:3, :] + cb_ref[...]

            a = conv(wa, cwa_ref, cba_ref)
            gt = conv(wg, cwg_ref, cbg_ref)
            o_ref[r0:r0 + g.bseq, :] = (_silu(gt) * a).astype(o_ref.dtype)


def _ffn_up(g: Geom, h, w_up, conv_w, conv_b, l, nfp):
    nf = w_up.shape[-1] // 2
    tn = _pick_tile(nf, FF_UP_TN)
    assert nfp % tn == 0
    n_real = nf // tn
    bm = g.bseq
    while bm * 2 <= FF_UP_BM and g.tp % (bm * 2) == 0 and g.ts % (bm * 2) == 0:
        bm *= 2

    def col(j):
        return jnp.minimum(j, n_real - 1)

    return pl.pallas_call(
        functools.partial(_ffn_up_kernel, g=g, n_real=n_real),
        grid=(g.t // bm, nfp // tn),
        in_specs=[
            pl.BlockSpec((bm, g.d), lambda i, j: (i, 0), pipeline_mode=pl.Buffered(1)),
            pl.BlockSpec((None, g.d, tn), lambda i, j: (l, 0, col(j))),
            pl.BlockSpec((None, g.d, tn), lambda i, j: (l, 0, n_real + col(j))),
            pl.BlockSpec((None, 3, tn), lambda i, j: (l, 0, col(j))),
            pl.BlockSpec((None, 3, tn), lambda i, j: (l, 0, n_real + col(j))),
            pl.BlockSpec((None, 1, tn), lambda i, j: (l, 0, col(j))),
            pl.BlockSpec((None, 1, tn), lambda i, j: (l, 0, n_real + col(j))),
        ],
        out_specs=pl.BlockSpec((bm, tn), lambda i, j: (i, j)),
        out_shape=jax.ShapeDtypeStruct((g.t, nfp), BF16),
        compiler_params=_params(2),
        name=f"ffn_up_l{l}",
    )(h, w_up, w_up, conv_w, conv_w, conv_b, conv_b)


def _pool_kernel(h_ref, w_ref, ps_ref, x_ref, g_ref, o_ref, *, g: Geom):
    bm = h_ref.shape[0]
    gi = pl.program_id(1)
    half = jnp.int32(0)
    for k, win in enumerate(POOL_WINDOWS):
        half = jnp.where(gi == k, win // 2, half)

    pos, length = _seq_position(g, bm)
    off = lax.broadcasted_iota(jnp.int32, (bm, bm), 1) - lax.broadcasted_iota(jnp.int32, (bm, bm), 0)
    src = pos + off
    band = (off >= -half) & (off < half) & (src >= 0) & (src < length)
    band = jnp.where(band, 1.0, 0.0).astype(BF16)
    cnt = (jnp.minimum(pos + half, length) - jnp.maximum(pos - half, 0)).astype(F32)

    h = h_ref[...]
    h_hi = h.astype(BF16)
    h_lo = (h - h_hi.astype(F32)).astype(BF16)
    wsum = (jnp.dot(band, h_hi, preferred_element_type=F32)
            + jnp.dot(band, h_lo, preferred_element_type=F32))
    pooled = (wsum / cnt - h).astype(BF16)
    y = jnp.dot(pooled, w_ref[...], preferred_element_type=F32) * ps_ref[...]
    o_ref[...] = x_ref[...] + g_ref[...] * y


def _pool_mixer(g: Geom, h, pool_w, pool_scale, x, mod, l, r):
    ng, pg = pool_w.shape[1], pool_w.shape[2]
    assert ng == len(POOL_WINDOWS) and pg % V7X_LANES == 0
    bm = g.bseq
    return pl.pallas_call(
        functools.partial(_pool_kernel, g=g),
        grid=(g.t // bm, ng),
        in_specs=[
            pl.BlockSpec((bm, pg), lambda i, gi: (i, gi)),
            pl.BlockSpec((None, None, pg, pg), lambda i, gi: (r, gi, 0, 0)),
            pl.BlockSpec((None, 1, pg), lambda i, gi: (r, 0, gi)),
            pl.BlockSpec((bm, pg), lambda i, gi: (i, gi)),
            _mod_spec(g, l, 2, bm, pg, col_axis=1),
        ],
        out_specs=pl.BlockSpec((bm, pg), lambda i, gi: (i, gi)),
        out_shape=jax.ShapeDtypeStruct((g.t, g.d), F32),
        compiler_params=_params(2),
        name=f"pool_mixer_l{l}",
    )(h, pool_w, pool_scale.reshape(pool_scale.shape[0], 1, g.d), x, mod)


def _ret_kernel(*refs, r, n_heads, seq, rope, has_s0, emit_state, n_aliased, hps):
    it = iter(refs)
    dls_ref = next(it)
    q_ref, k_ref, v_ref, gate_ref = next(it), next(it), next(it), next(it)
    cos_ref, sin_ref = (next(it), next(it)) if rope else (None, None)
    s0_ref = next(it) if has_s0 else None
    for _ in range(n_aliased):
        next(it)
    o_ref = next(it)
    st_ref = next(it) if emit_state else None
    oacc_ref = next(it)

    hk = q_ref.shape[1] // hps
    hv = v_ref.shape[1] // hps
    hh = hk // 2
    n_chunks = seq // CHUNK

    ri = lax.broadcasted_iota(jnp.int32, (CHUNK, CHUNK), 0).astype(F32)
    ci = lax.broadcasted_iota(jnp.int32, (CHUNK, CHUNK), 1).astype(F32)
    diff = ri - ci
    idx = lax.broadcasted_iota(jnp.int32, (CHUNK, 1), 0).astype(F32)

    def rot(x, rows):
        parts = []
        for p in range(2):
            xp = x[:, p * hh:(p + 1) * hh]
            parts.append(xp * cos_ref[rows, p * hh:(p + 1) * hh]
                         + pltpu.roll(xp, hh // 2, axis=1) * sin_ref[rows, p * hh:(p + 1) * hh])
        return jnp.concatenate(parts, axis=1)

    def cross(qc, s, qdec):
        return jnp.dot(qc, s.astype(BF16), preferred_element_type=F32) * qdec

    def advance(s, kf, kdec, cdec, vc):
        kd = (kf * kdec).astype(BF16)
        return s * cdec + lax.dot_general(kd, vc, (((0,), (0,)), ((), ())), preferred_element_type=F32)

    for hd in range(hps):
        head = pl.program_id(1) * hps + hd
        kcol = slice(hd * hk, (hd + 1) * hk)
        vcol = slice(hd * hv, (hd + 1) * hv)

        def log_gamma(direction):
            ls = dls_ref[(r * 2 + direction) * n_heads + head]
            return -jnp.exp(jnp.full((1, 1), ls, F32))

        lg_f, lg_b = log_gamma(0), log_gamma(1)
        intra = (jnp.where(diff >= 0, jnp.exp(lg_f * jnp.maximum(diff, 0.0)), 0.0)
                 + jnp.where(diff <= 0, jnp.exp(lg_b * jnp.maximum(-diff, 0.0)), 0.0))
        qdec_f = jnp.exp(lg_f * (idx + 1.0))
        kdec_f = jnp.exp(lg_f * (CHUNK - 1.0 - idx))
        qdec_b = jnp.exp(lg_b * (CHUNK - idx))
        kdec_b = jnp.exp(lg_b * idx)
        cdec_f = jnp.exp(lg_f * CHUNK)
        cdec_b = jnp.exp(lg_b * CHUNK)

        qs, kfs = [], []
        for c in range(n_chunks):
            rows = slice(c * CHUNK, (c + 1) * CHUNK)
            q = q_ref[rows, kcol]
            kf = k_ref[rows, kcol].astype(F32) * (hk ** -0.5)
            if rope:
                q = rot(q.astype(F32), rows).astype(BF16)
                kf = rot(kf, rows)
            qs.append(q)
            kfs.append(kf)

        s = s0_ref[0, hd] if has_s0 else jnp.zeros((hk, hv), F32)
        for c in range(n_chunks):
            rows = slice(c * CHUNK, (c + 1) * CHUNK)
            vc = v_ref[rows, vcol]
            scores = lax.dot_general(qs[c], kfs[c].astype(BF16), (((1,), (1,)), ((), ())),
                                     preferred_element_type=F32)
            o = jnp.dot((scores * intra).astype(BF16), vc, preferred_element_type=F32)
            oacc_ref[rows, vcol] = o + cross(qs[c], s, qdec_f)
            s = advance(s, kfs[c], kdec_f, cdec_f, vc)
        if emit_state:
            st_ref[0, hd] = s

        s = s0_ref[1, hd] if has_s0 else jnp.zeros((hk, hv), F32)
        for c in reversed(range(n_chunks)):
            rows = slice(c * CHUNK, (c + 1) * CHUNK)
            vc = v_ref[rows, vcol]
            o = oacc_ref[rows, vcol] + cross(qs[c], s, qdec_b)
            mu = jnp.mean(o, axis=-1, keepdims=True)
            dev = o - mu
            var = jnp.mean(dev * dev, axis=-1, keepdims=True)
            normed = dev * lax.rsqrt(var + EPS)
            gate = gate_ref[rows, vcol].astype(F32)
            o_ref[rows, vcol] = (_silu(gate) * normed).astype(o_ref.dtype)
            s = advance(s, kfs[c], kdec_b, cdec_b, vc)
        if emit_state:
            st_ref[1, hd] = s


def _retention(proj, decay_ls, r, n_heads, row0, n_seq, seq, o_buf=None, rope_tables=None, s0=None,
               state_shape=None, state_buf=None):
    t, ret_in = proj.shape
    hk = ret_in // (6 * n_heads)
    hv = 2 * hk
    assert seq % CHUNK == 0 and row0 % seq == 0
    rb0 = row0 // seq
    rope = rope_tables is not None
    has_s0 = s0 is not None
    emit_state = state_shape is not None

    hps = RET_HEADS_PER_STEP
    assert n_heads % hps == 0
    nhb = n_heads // hps
    state_spec = pl.BlockSpec((None, None, 2, hps, hk, hv), lambda b, h, *_: (b, r, 0, h, 0, 0))

    in_specs = [
        pl.BlockSpec((seq, hps * hk), lambda b, h, *_: (rb0 + b, h)),
        pl.BlockSpec((seq, hps * hk), lambda b, h, *_: (rb0 + b, nhb + h)),
        pl.BlockSpec((seq, hps * hv), lambda b, h, *_: (rb0 + b, nhb + h)),
        pl.BlockSpec((seq, hps * hv), lambda b, h, *_: (rb0 + b, 2 * nhb + h)),
    ]
    args = [proj, proj, proj, proj]
    if rope:
        in_specs += [pl.BlockSpec((seq, hk), lambda b, h, *_: (0, 0))] * 2
        args += list(rope_tables)
    if has_s0:
        in_specs.append(state_spec)
        args.append(s0)
    aliases = {}
    out_specs = [pl.BlockSpec((seq, hps * hv), lambda b, h, *_: (rb0 + b, h))]
    out_shape = [jax.ShapeDtypeStruct((t, n_heads * hv), BF16)]
    if emit_state:
        out_specs.append(state_spec)
        out_shape.append(jax.ShapeDtypeStruct(state_shape, F32))
    for out_idx, buf in enumerate((o_buf, state_buf)):
        if buf is not None:
            aliases[1 + len(args)] = out_idx
            in_specs.append(pl.BlockSpec(memory_space=pl.ANY))
            args.append(buf)

    out = pl.pallas_call(
        functools.partial(_ret_kernel, r=r, n_heads=n_heads, seq=seq, rope=rope, has_s0=has_s0,
                          emit_state=emit_state, n_aliased=len(aliases), hps=hps),
        grid_spec=pltpu.PrefetchScalarGridSpec(
            num_scalar_prefetch=1,
            grid=(n_seq, nhb),
            in_specs=in_specs,
            out_specs=out_specs,
            scratch_shapes=[pltpu.VMEM((seq, hps * hv), F32)],
        ),
        out_shape=out_shape,
        input_output_aliases=aliases,
        compiler_params=_params(2),
        name=f"retention_r{r}_{'latent' if rope else 'prompt'}",
    )(decay_ls.reshape(-1), *args)
    return out if emit_state else (out[0], None)


def _rope_tables(seq, hk):
    rows = seq // GRID_W
    row_ids = jnp.repeat(jnp.arange(rows), GRID_W).astype(F32)
    col_ids = jnp.tile(jnp.arange(GRID_W), rows).astype(F32)
    n = hk // 4
    freqs = ROPE_BASE ** (-jnp.arange(n, dtype=F32) / n)
    cos, sin = [], []
    for ang in (row_ids[:, None] * freqs, col_ids[:, None] * freqs):
        c, s = jnp.cos(ang), jnp.sin(ang)
        cos += [c, c]
        sin += [-s, s]
    return jnp.concatenate(cos, axis=1), jnp.concatenate(sin, axis=1)


def _pad_ff(a, nf, nfp, axis):
    pad = [(0, 0)] * a.ndim
    pad[axis] = (0, nfp - nf)
    return jnp.pad(a, pad)


def kernel(x_prompt, x_sample, state_ret, c, c_ctx, mod_w, mod_b, norm_g, final_norm_g, pool_w, pool_scale,
           ret_w_in, ret_w_out, ret_decay_ls, ffn_w_up, ffn_conv_w, ffn_conv_b, ffn_w_down):
    b, s, d = x_prompt.shape
    db, ds, _ = x_sample.shape
    depth = mod_w.shape[0]
    n_heads = ret_decay_ls.shape[-1]
    n_ret = ret_w_in.shape[0]
    hk = d // n_heads
    hv = ret_w_out.shape[1] // n_heads
    nf = ffn_w_down.shape[1]
    nfp = -(-nf // FF_ALIGN) * FF_ALIGN

    bseq = max(s, ds)
    g = Geom(tp=b * s, ts=db * ds, s=s, ds=ds, d=d, bseq=bseq)
    assert bseq % s == 0 and bseq % ds == 0 and g.tp % bseq == 0 and g.ts % bseq == 0
    assert g.tp % ds == 0 and s % ROW_TILE == 0 and ds % ROW_TILE == 0
    assert 1 + db <= N_COND_ROWS and hv == 2 * hk and ds % GRID_W == 0

    pool_w_b = pool_w.astype(BF16)
    w_out_b = ret_w_out.astype(BF16)
    w_down_b = _pad_ff(ffn_w_down.astype(BF16), nf, nfp, 1)
    conv_b = ffn_conv_b.reshape(depth, 1, 2 * nf)

    cond = jnp.concatenate([c_ctx[None, :], c, jnp.zeros((N_COND_ROWS - 1 - db, d), F32)], axis=0)
    mod = _modulation(cond, mod_w, mod_b)

    x = jnp.concatenate([x_prompt.reshape(g.tp, d), x_sample.reshape(g.ts, d)], axis=0)
    rope_tables = _rope_tables(ds, hk)
    state_shape = (b, n_ret, 2, n_heads, hk, hv)
    state_buf = None

    for l in range(depth):
        r = l // N_MIXERS
        if l % N_MIXERS == 0:
            h = _norm_mod(g, x, norm_g, mod, l, 0, F32)
            x = _pool_mixer(g, h, pool_w_b, pool_scale, x, mod, l, r)
        else:
            h = _norm_mod(g, x, norm_g, mod, l, 0, BF16)
            proj = _matmul(h, ret_w_in, r, BF16, f"ret_in_l{l}")
            o_buf, state_buf = _retention(proj, ret_decay_ls, r, n_heads, 0, b, s,
                                          state_shape=state_shape, state_buf=state_buf)
            o_buf, _ = _retention(proj, ret_decay_ls, r, n_heads, g.tp, db, ds, o_buf=o_buf,
                                  rope_tables=rope_tables, s0=state_ret)
            x = _matmul_residual(g, o_buf, w_out_b, r, x, mod, l, 2, f"ret_out_l{l}")
        h = _norm_mod(g, x, norm_g, mod, l, 1, BF16)
        act = _ffn_up(g, h, ffn_w_up, ffn_conv_w, conv_b, l, nfp)
        x = _matmul_residual(g, act, w_down_b, l, x, mod, l, 5, f"ffn_down_l{l}")

    y_prompt = _final_norm(g, x, final_norm_g, 0, g.tp).reshape(b, s, d)
    y_sample = _final_norm(g, x, final_norm_g, g.tp, g.ts).reshape(db, ds, d)
    return y_prompt, y_sample, state_buf
```

```python
import functools
from typing import NamedTuple

import jax
import jax.numpy as jnp
from jax import lax
from jax.experimental import pallas as pl
from jax.experimental.pallas import tpu as pltpu

F32 = jnp.float32
BF16 = jnp.bfloat16

EPS = 1e-6
CHUNK = 256
GRID_W = 64
ROPE_BASE = 10000.0
POOL_WINDOWS = (2, 4, 8, 16)
N_MIXERS = 2

V7X_LANES = 128
V7X_VMEM_BYTES = 64 * 1024 * 1024
VMEM_LIMIT = V7X_VMEM_BYTES - 8 * 1024 * 1024
N_COND_ROWS = 16
ROW_TILE = 512
MM_TILE = 1024
FF_ALIGN = 1024
FF_UP_TN = 256
FF_UP_BM = 3072
N_K_STEPS = 4
RET_HEADS_PER_STEP = 2
POOL_BLOCK = 128
CAST_ROWS = 256


class Geom(NamedTuple):
    tp: int
    ts: int
    s: int
    ds: int
    d: int
    bseq: int

    @property
    def t(self):
        return self.tp + self.ts


def _cond_index(g: Geom, i, bm):
    return jnp.maximum(i * bm - g.tp + g.ds, 0) // g.ds


def _mod_spec(g: Geom, l, which, bm, bn=None, col_axis=None):
    bn = g.d if bn is None else bn

    def index(*ids):
        j = 0 if col_axis is None else ids[col_axis]
        return (l, which, _cond_index(g, ids[0], bm), 0, j)

    return pl.BlockSpec((None, None, None, 1, bn), index)


def _params(n_axes, vmem_limit=VMEM_LIMIT):
    return pltpu.CompilerParams(dimension_semantics=("arbitrary",) * n_axes,
                                vmem_limit_bytes=vmem_limit)


def _pick_tile(n, cap, align=V7X_LANES):
    best = None
    for t in range(align, min(n, cap) + 1, align):
        if n % t == 0:
            best = t
    assert best is not None, (n, cap)
    return best


def _rem_static(x, n):
    return x & (n - 1) if n & (n - 1) == 0 else lax.rem(x, n)


def _silu(x):
    return x * jax.nn.sigmoid(x)


def _mod_kernel(c_ref, w_ref, b_ref, o_ref):
    a = _silu(c_ref[...]).astype(BF16)
    acc = jnp.dot(a, w_ref[...].astype(BF16), preferred_element_type=F32)
    o_ref[...] = acc + b_ref[...]


def _modulation(cond, mod_w, mod_b):
    depth, d, n6 = mod_w.shape
    tn = _pick_tile(n6, 512)
    out = pl.pallas_call(
        _mod_kernel,
        grid=(depth, n6 // tn),
        in_specs=[
            pl.BlockSpec((N_COND_ROWS, d), lambda l, j: (0, 0)),
            pl.BlockSpec((None, d, tn), lambda l, j: (l, 0, j)),
            pl.BlockSpec((None, 1, tn), lambda l, j: (l, 0, j)),
        ],
        out_specs=pl.BlockSpec((None, N_COND_ROWS, tn), lambda l, j: (l, 0, j)),
        out_shape=jax.ShapeDtypeStruct((depth, N_COND_ROWS, n6), F32),
        compiler_params=_params(2),
        name="adaln_modulation",
    )(cond, mod_w, mod_b.reshape(depth, 1, n6))
    out = out.reshape(depth, N_COND_ROWS, 6, 1, d)
    return jnp.transpose(out, (0, 2, 1, 3, 4))


def _to_bf16_kernel(w_ref, o_ref, *, n_real):
    i = pl.program_id(1)

    @pl.when(i < n_real)
    def _():
        o_ref[...] = w_ref[...].astype(BF16)

    @pl.when(i >= n_real)
    def _():
        o_ref[...] = jnp.zeros_like(o_ref)


def _to_bf16(w, rows_padded=None):
    nl, k, n = w.shape
    kp = k if rows_padded is None else rows_padded
    tk = _pick_tile(k, CAST_ROWS, align=16)
    assert kp % tk == 0
    n_real = k // tk
    return pl.pallas_call(
        functools.partial(_to_bf16_kernel, n_real=n_real),
        grid=(nl, kp // tk),
        in_specs=[pl.BlockSpec((None, tk, n), lambda l, i: (l, jnp.minimum(i, n_real - 1), 0))],
        out_specs=pl.BlockSpec((None, tk, n), lambda l, i: (l, i, 0)),
        out_shape=jax.ShapeDtypeStruct((nl, kp, n), BF16),
        compiler_params=_params(2),
        name=f"to_bf16_{nl}x{k}x{n}",
    )(w)


def _norm_mod_kernel(x_ref, g_ref, sc_ref, sh_ref, o_ref, *, j):
    x = x_ref[...]
    y = x * lax.rsqrt(jnp.mean(x * x, axis=-1, keepdims=True) + EPS)
    y = y * g_ref[j:j + 1, :]
    o_ref[...] = (y * (1.0 + sc_ref[...]) + sh_ref[...]).astype(o_ref.dtype)


def _norm_mod(g: Geom, x, norm_g, mod, l, j, out_dtype):
    bm = ROW_TILE
    return pl.pallas_call(
        functools.partial(_norm_mod_kernel, j=j),
        grid=(g.t // bm,),
        in_specs=[
            pl.BlockSpec((bm, g.d), lambda i: (i, 0)),
            pl.BlockSpec((None, 2, g.d), lambda i: (l, 0, 0)),
            _mod_spec(g, l, 3 * j + 1, bm),
            _mod_spec(g, l, 3 * j, bm),
        ],
        out_specs=pl.BlockSpec((bm, g.d), lambda i: (i, 0)),
        out_shape=jax.ShapeDtypeStruct((g.t, g.d), out_dtype),
        compiler_params=_params(1),
        name=f"norm_mod_l{l}_{j}",
    )(x, norm_g, mod, mod)


def _final_norm_kernel(x_ref, g_ref, o_ref):
    x = x_ref[...]
    y = x * lax.rsqrt(jnp.mean(x * x, axis=-1, keepdims=True) + EPS)
    o_ref[...] = y * g_ref[...]


def _final_norm(g: Geom, x, final_g, row0, rows):
    bm = ROW_TILE
    blk0 = row0 // bm
    return pl.pallas_call(
        _final_norm_kernel,
        grid=(rows // bm,),
        in_specs=[
            pl.BlockSpec((bm, g.d), lambda i: (blk0 + i, 0)),
            pl.BlockSpec((1, g.d), lambda i: (0, 0)),
        ],
        out_specs=pl.BlockSpec((bm, g.d), lambda i: (i, 0)),
        out_shape=jax.ShapeDtypeStruct((rows, g.d), F32),
        compiler_params=_params(1),
        name=f"final_norm_{row0}",
    )(x, final_g.reshape(1, g.d))


def _mm_plain_kernel(a_ref, b_ref, o_ref):
    b = b_ref[...].astype(BF16)
    o_ref[...] = jnp.dot(a_ref[...], b, preferred_element_type=F32).astype(o_ref.dtype)


def _matmul(a, w, r, out_dtype, name):
    t, k = a.shape
    n = w.shape[-1]
    bm = _pick_tile(t, 2 * MM_TILE)
    bn = _pick_tile(n, MM_TILE // 2)
    return pl.pallas_call(
        _mm_plain_kernel,
        grid=(t // bm, n // bn),
        in_specs=[
            pl.BlockSpec((bm, k), lambda i, j: (i, 0), pipeline_mode=pl.Buffered(1)),
            pl.BlockSpec((None, k, bn), lambda i, j: (r, 0, j)),
        ],
        out_specs=pl.BlockSpec((bm, bn), lambda i, j: (i, j)),
        out_shape=jax.ShapeDtypeStruct((t, n), out_dtype),
        compiler_params=_params(2),
        name=name,
    )(a, w)


def _mm_residual_kernel(*refs, nk, prompt_tiles):
    *a_refs, b_ref, x_ref, g_ref, o_ref, acc_ref = refs
    k = pl.program_id(2)

    @pl.when((pl.program_id(0) == 0) & (pl.program_id(1) == 0) & (k == 0))
    def _():
        acc_ref[...] = jnp.zeros_like(acc_ref)

    if len(a_refs) == 1:
        a = a_refs[0][...]
    else:
        a = jnp.where(pl.program_id(0) < prompt_tiles, a_refs[0][...], a_refs[1][...])

    acc = acc_ref[...] + jnp.dot(a, b_ref[...], preferred_element_type=F32)
    acc_ref[...] = jnp.where(k == nk - 1, 0.0, acc)
    o_ref[...] = x_ref[...] + g_ref[...] * acc


def _matmul_residual(g: Geom, a, w, r, x, mod, l, gate, name):
    t, n = x.shape
    bm = _pick_tile(g.bseq, MM_TILE)
    bn = _pick_tile(n, MM_TILE)
    nk = N_K_STEPS
    k = w.shape[-2]
    bk = k // nk
    assert k % nk == 0 and bk % V7X_LANES == 0, (k, nk)
    npt = g.tp // bm
    if isinstance(a, tuple):
        a_specs = [
            pl.BlockSpec((bm, bk), lambda i, j, kk: (jnp.minimum(i, npt - 1), jnp.where(i < npt, kk, nk - 1))),
            pl.BlockSpec((bm, bk), lambda i, j, kk: (jnp.maximum(i - npt, 0), jnp.where(i < npt, 0, kk))),
        ]
        a_args = list(a)
    else:
        a_specs = [pl.BlockSpec((bm, bk), lambda i, j, kk: (i, kk))]
        a_args = [a]
    return pl.pallas_call(
        functools.partial(_mm_residual_kernel, nk=nk, prompt_tiles=npt),
        grid=(t // bm, n // bn, nk),
        in_specs=a_specs + [
            pl.BlockSpec((None, bk, bn), lambda i, j, kk: (r, kk, j)),
            pl.BlockSpec((bm, bn), lambda i, j, kk: (i, j)),
            _mod_spec(g, l, gate, bm, bn, col_axis=1),
        ],
        out_specs=pl.BlockSpec((bm, bn), lambda i, j, kk: (i, j)),
        out_shape=jax.ShapeDtypeStruct((t, n), F32),
        scratch_shapes=[pltpu.VMEM((bm, bn), F32)],
        compiler_params=_params(3),
        name=name,
    )(*a_args, w, x, mod)


def _seq_position(g: Geom, bm, r0=0):
    is_prompt = pl.program_id(0) * bm + r0 < g.tp
    t = lax.broadcasted_iota(jnp.int32, (g.bseq, 1), 0)
    pos = jnp.where(is_prompt, _rem_static(t, g.s), _rem_static(t, g.ds))
    length = jnp.where(is_prompt, g.s, g.ds)
    return pos, length


def _ffn_up_kernel(h_ref, wa_ref, wg_ref, cwa_ref, cwg_ref, cba_ref, cbg_ref, o_ref, *, g: Geom, n_real):
    bm = h_ref.shape[0]
    j = pl.program_id(1)

    @pl.when(j >= n_real)
    def _():
        o_ref[...] = jnp.zeros_like(o_ref)

    @pl.when(j < n_real)
    def _():
        wa = wa_ref[...].astype(BF16)
        wg = wg_ref[...].astype(BF16)

        for r0 in range(0, bm, g.bseq):
            h = h_ref[r0:r0 + g.bseq, :]
            pos, length = _seq_position(g, bm, r0)
            first = pos == 0
            last = pos == length - 1

            def conv(w, cw_ref, cb_ref):
                u = jnp.dot(h, w, preferred_element_type=F32)
                prev = jnp.where(first, 0.0, pltpu.roll(u, 1, axis=0))
                nxt = jnp.where(last, 0.0, pltpu.roll(u, g.bseq - 1, axis=0))
                return prev * cw_ref[0:1, :] + u * cw_ref[1:2, :] + nxt * cw_ref[2:3, :] + cb_ref[...]

            a = conv(wa, cwa_ref, cba_ref)
            gt = conv(wg, cwg_ref, cbg_ref)
            o_ref[r0:r0 + g.bseq, :] = (_silu(gt) * a).astype(o_ref.dtype)


def _ffn_up(g: Geom, h, w_up, conv_w, conv_b, l, nfp):
    nf = w_up.shape[-1] // 2
    tn = _pick_tile(nf, FF_UP_TN)
    assert nfp % tn == 0
    n_real = nf // tn
    bm = max(m * g.bseq for m in range(1, FF_UP_BM // g.bseq + 1) if g.t % (m * g.bseq) == 0)

    def col(j):
        return jnp.minimum(j, n_real - 1)

    return pl.pallas_call(
        functools.partial(_ffn_up_kernel, g=g, n_real=n_real),
        grid=(g.t // bm, nfp // tn),
        in_specs=[
            pl.BlockSpec((bm, g.d), lambda i, j: (i, 0), pipeline_mode=pl.Buffered(1)),
            pl.BlockSpec((None, g.d, tn), lambda i, j: (l, 0, col(j))),
            pl.BlockSpec((None, g.d, tn), lambda i, j: (l, 0, n_real + col(j))),
            pl.BlockSpec((None, 3, tn), lambda i, j: (l, 0, col(j))),
            pl.BlockSpec((None, 3, tn), lambda i, j: (l, 0, n_real + col(j))),
            pl.BlockSpec((None, 1, tn), lambda i, j: (l, 0, col(j))),
            pl.BlockSpec((None, 1, tn), lambda i, j: (l, 0, n_real + col(j))),
        ],
        out_specs=pl.BlockSpec((bm, tn), lambda i, j: (i, j)),
        out_shape=jax.ShapeDtypeStruct((g.t, nfp), BF16),
        compiler_params=_params(2),
        name=f"ffn_up_l{l}",
    )(h, w_up, w_up, conv_w, conv_w, conv_b, conv_b)


def _pool_kernel(h_ref, w_ref, ps_ref, x_ref, g_ref, o_ref, *, g: Geom):
    bm = h_ref.shape[0]
    gi = pl.program_id(1)
    half = jnp.int32(0)
    for k, win in enumerate(POOL_WINDOWS):
        half = jnp.where(gi == k, win // 2, half)

    pos, length = _seq_position(g, bm)
    cnt = (jnp.minimum(pos + half, length) - jnp.maximum(pos - half, 0)).astype(F32)

    h = h_ref[...]
    h_hi = h.astype(BF16)
    h_lo = (h - h_hi.astype(F32)).astype(BF16)

    nb = bm // POOL_BLOCK
    sums = []
    for rb in range(nb):
        r0, c0, c1 = rb * POOL_BLOCK, max(rb - 1, 0) * POOL_BLOCK, min(rb + 2, nb) * POOL_BLOCK
        off = ((c0 - r0) + lax.broadcasted_iota(jnp.int32, (POOL_BLOCK, c1 - c0), 1)
               - lax.broadcasted_iota(jnp.int32, (POOL_BLOCK, c1 - c0), 0))
        src = pos[r0:r0 + POOL_BLOCK] + off
        band = (off >= -half) & (off < half) & (src >= 0) & (src < length)
        band = jnp.where(band, 1.0, 0.0).astype(BF16)
        sums.append(jnp.dot(band, h_hi[c0:c1], preferred_element_type=F32)
                    + jnp.dot(band, h_lo[c0:c1], preferred_element_type=F32))
    wsum = jnp.concatenate(sums, axis=0)
    pooled = (wsum / cnt - h).astype(BF16)
    y = jnp.dot(pooled, w_ref[...], preferred_element_type=F32) * ps_ref[...]
    o_ref[...] = x_ref[...] + g_ref[...] * y


def _pool_mixer(g: Geom, h, pool_w, pool_scale, x, mod, l, r):
    ng, pg = pool_w.shape[1], pool_w.shape[2]
    assert ng == len(POOL_WINDOWS) and pg % V7X_LANES == 0
    assert max(POOL_WINDOWS) // 2 <= POOL_BLOCK and g.bseq % POOL_BLOCK == 0
    bm = g.bseq
    return pl.pallas_call(
        functools.partial(_pool_kernel, g=g),
        grid=(g.t // bm, ng),
        in_specs=[
            pl.BlockSpec((bm, pg), lambda i, gi: (i, gi)),
            pl.BlockSpec((None, None, pg, pg), lambda i, gi: (r, gi, 0, 0)),
            pl.BlockSpec((None, 1, pg), lambda i, gi: (r, 0, gi)),
            pl.BlockSpec((bm, pg), lambda i, gi: (i, gi)),
            _mod_spec(g, l, 2, bm, pg, col_axis=1),
        ],
        out_specs=pl.BlockSpec((bm, pg), lambda i, gi: (i, gi)),
        out_shape=jax.ShapeDtypeStruct((g.t, g.d), F32),
        compiler_params=_params(2),
        name=f"pool_mixer_l{l}",
    )(h, pool_w, pool_scale.reshape(pool_scale.shape[0], 1, g.d), x, mod)


def _ret_kernel(*refs, r, n_heads, seq, rope, has_s0, emit_state, fill_state, n_aliased, hps):
    it = iter(refs)
    dls_ref = next(it)
    q_ref, k_ref, v_ref, gate_ref = next(it), next(it), next(it), next(it)
    cos_ref, sin_ref = (next(it), next(it)) if rope else (None, None)
    s0_ref = next(it) if has_s0 else None
    for _ in range(n_aliased):
        next(it)
    o_ref = next(it)
    st_ref = next(it) if emit_state else None
    oacc_ref = next(it)

    hk = q_ref.shape[1] // hps
    hv = v_ref.shape[1] // hps
    hh = hk // 2
    n_chunks = seq // CHUNK

    if fill_state:
        for layer in range(st_ref.shape[0]):
            if layer != r:
                st_ref[layer] = jnp.zeros(st_ref.shape[1:], F32)
        st_ref = st_ref.at[r]

    ri = lax.broadcasted_iota(jnp.int32, (CHUNK, CHUNK), 0).astype(F32)
    ci = lax.broadcasted_iota(jnp.int32, (CHUNK, CHUNK), 1).astype(F32)
    diff = ri - ci
    idx = lax.broadcasted_iota(jnp.int32, (CHUNK, 1), 0).astype(F32)

    def rot(x, rows):
        parts = []
        for p in range(2):
            xp = x[:, p * hh:(p + 1) * hh]
            parts.append(xp * cos_ref[rows, p * hh:(p + 1) * hh]
                         + pltpu.roll(xp, hh // 2, axis=1) * sin_ref[rows, p * hh:(p + 1) * hh])
        return jnp.concatenate(parts, axis=1)

    def add_cross(o, qc, s, qdec):
        if s is None:
            return o
        return o + jnp.dot(qc, s.astype(BF16), preferred_element_type=F32) * qdec

    def advance(s, kf, kdec, cdec, vc):
        kd = (kf * kdec).astype(BF16)
        kv = lax.dot_general(kd, vc, (((0,), (0,)), ((), ())), preferred_element_type=F32)
        return kv if s is None else s * cdec + kv

    for hd in range(hps):
        head = pl.program_id(1) * hps + hd
        kcol = slice(hd * hk, (hd + 1) * hk)
        vcol = slice(hd * hv, (hd + 1) * hv)

        def log_gamma(direction):
            ls = dls_ref[(r * 2 + direction) * n_heads + head]
            return -jnp.exp(jnp.full((1, 1), ls, F32))

        lg_f, lg_b = log_gamma(0), log_gamma(1)
        intra = (jnp.where(diff >= 0, jnp.exp(lg_f * jnp.maximum(diff, 0.0)), 0.0)
                 + jnp.where(diff <= 0, jnp.exp(lg_b * jnp.maximum(-diff, 0.0)), 0.0))
        qdec_f = jnp.exp(lg_f * (idx + 1.0))
        kdec_f = jnp.exp(lg_f * (CHUNK - 1.0 - idx))
        qdec_b = jnp.exp(lg_b * (CHUNK - idx))
        kdec_b = jnp.exp(lg_b * idx)
        cdec_f = jnp.exp(lg_f * CHUNK)
        cdec_b = jnp.exp(lg_b * CHUNK)

        qs, kfs = [], []
        for c in range(n_chunks):
            rows = slice(c * CHUNK, (c + 1) * CHUNK)
            q = q_ref[rows, kcol]
            kf = k_ref[rows, kcol].astype(F32) * (hk ** -0.5)
            if rope:
                q = rot(q.astype(F32), rows).astype(BF16)
                kf = rot(kf, rows)
            qs.append(q)
            kfs.append(kf)

        s = s0_ref[0, hd] if has_s0 else None
        for c in range(n_chunks):
            rows = slice(c * CHUNK, (c + 1) * CHUNK)
            vc = v_ref[rows, vcol]
            scores = lax.dot_general(qs[c], kfs[c].astype(BF16), (((1,), (1,)), ((), ())),
                                     preferred_element_type=F32)
            o = jnp.dot((scores * intra).astype(BF16), vc, preferred_element_type=F32)
            oacc_ref[rows, vcol] = add_cross(o, qs[c], s, qdec_f)
            if emit_state or c + 1 < n_chunks:
                s = advance(s, kfs[c], kdec_f, cdec_f, vc)
        if emit_state:
            st_ref[0, hd] = s

        s = s0_ref[1, hd] if has_s0 else None
        for c in reversed(range(n_chunks)):
            rows = slice(c * CHUNK, (c + 1) * CHUNK)
            vc = v_ref[rows, vcol]
            o = add_cross(oacc_ref[rows, vcol], qs[c], s, qdec_b)
            mu = jnp.mean(o, axis=-1, keepdims=True)
            dev = o - mu
            var = jnp.mean(dev * dev, axis=-1, keepdims=True)
            normed = dev * lax.rsqrt(var + EPS)
            gate = gate_ref[rows, vcol].astype(F32)
            o_ref[rows, vcol] = (_silu(gate) * normed).astype(o_ref.dtype)
            if emit_state or c > 0:
                s = advance(s, kfs[c], kdec_b, cdec_b, vc)
        if emit_state:
            st_ref[1, hd] = s


def _retention(proj, decay_ls, r, n_heads, row0, n_seq, seq, rope_tables=None, s0=None,
               state_shape=None, state_buf=None):
    t, ret_in = proj.shape
    hk = ret_in // (6 * n_heads)
    hv = 2 * hk
    assert seq % CHUNK == 0 and row0 % seq == 0
    rb0 = row0 // seq
    rope = rope_tables is not None
    has_s0 = s0 is not None
    emit_state = state_shape is not None
    fill_state = emit_state and state_buf is None

    hps = RET_HEADS_PER_STEP
    assert n_heads % hps == 0
    nhb = n_heads // hps
    state_spec = pl.BlockSpec((None, None, 2, hps, hk, hv), lambda b, h, *_: (b, r, 0, h, 0, 0))

    in_specs = [
        pl.BlockSpec((seq, hps * hk), lambda b, h, *_: (rb0 + b, h)),
        pl.BlockSpec((seq, hps * hk), lambda b, h, *_: (rb0 + b, nhb + h)),
        pl.BlockSpec((seq, hps * hv), lambda b, h, *_: (rb0 + b, nhb + h)),
        pl.BlockSpec((seq, hps * hv), lambda b, h, *_: (rb0 + b, 2 * nhb + h)),
    ]
    args = [proj, proj, proj, proj]
    if rope:
        in_specs += [pl.BlockSpec((seq, hk), lambda b, h, *_: (0, 0))] * 2
        args += list(rope_tables)
    if has_s0:
        in_specs.append(state_spec)
        args.append(s0)
    aliases = {}
    out_specs = [pl.BlockSpec((seq, hps * hv), lambda b, h, *_: (b, h))]
    out_shape = [jax.ShapeDtypeStruct((n_seq * seq, n_heads * hv), BF16)]
    if emit_state:
        if fill_state:
            out_specs.append(pl.BlockSpec((None, state_shape[1], 2, hps, hk, hv),
                                          lambda b, h, *_: (b, 0, 0, h, 0, 0)))
        else:
            out_specs.append(state_spec)
            aliases[1 + len(args)] = 1
            in_specs.append(pl.BlockSpec(memory_space=pl.ANY))
            args.append(state_buf)
        out_shape.append(jax.ShapeDtypeStruct(state_shape, F32))

    out = pl.pallas_call(
        functools.partial(_ret_kernel, r=r, n_heads=n_heads, seq=seq, rope=rope, has_s0=has_s0,
                          emit_state=emit_state, fill_state=fill_state, n_aliased=len(aliases), hps=hps),
        grid_spec=pltpu.PrefetchScalarGridSpec(
            num_scalar_prefetch=1,
            grid=(n_seq, nhb),
            in_specs=in_specs,
            out_specs=out_specs,
            scratch_shapes=[pltpu.VMEM((seq, hps * hv), F32)],
        ),
        out_shape=out_shape,
        input_output_aliases=aliases,
        compiler_params=_params(2),
        name=f"retention_r{r}_{'latent' if rope else 'prompt'}",
    )(decay_ls.reshape(-1), *args)
    return out if emit_state else (out[0], None)


def _rope_tables(seq, hk):
    rows = seq // GRID_W
    row_ids = jnp.repeat(jnp.arange(rows), GRID_W).astype(F32)
    col_ids = jnp.tile(jnp.arange(GRID_W), rows).astype(F32)
    n = hk // 4
    freqs = ROPE_BASE ** (-jnp.arange(n, dtype=F32) / n)
    cos, sin = [], []
    for ang in (row_ids[:, None] * freqs, col_ids[:, None] * freqs):
        c, s = jnp.cos(ang), jnp.sin(ang)
        cos += [c, c]
        sin += [-s, s]
    return jnp.concatenate(cos, axis=1), jnp.concatenate(sin, axis=1)


def kernel(x_prompt, x_sample, state_ret, c, c_ctx, mod_w, mod_b, norm_g, final_norm_g, pool_w, pool_scale,
           ret_w_in, ret_w_out, ret_decay_ls, ffn_w_up, ffn_conv_w, ffn_conv_b, ffn_w_down):
    b, s, d = x_prompt.shape
    db, ds, _ = x_sample.shape
    depth = mod_w.shape[0]
    n_heads = ret_decay_ls.shape[-1]
    n_ret = ret_w_in.shape[0]
    hk = d // n_heads
    hv = ret_w_out.shape[1] // n_heads
    nf = ffn_w_down.shape[1]
    nfp = -(-nf // FF_ALIGN) * FF_ALIGN

    bseq = max(s, ds)
    g = Geom(tp=b * s, ts=db * ds, s=s, ds=ds, d=d, bseq=bseq)
    assert bseq % s == 0 and bseq % ds == 0 and g.tp % bseq == 0 and g.ts % bseq == 0
    assert g.tp % ds == 0 and g.tp % ROW_TILE == 0 and ds % ROW_TILE == 0
    assert 1 + db <= N_COND_ROWS and hv == 2 * hk and ds % GRID_W == 0

    pool_w_b = _to_bf16(pool_w.reshape((-1,) + pool_w.shape[2:])).reshape(pool_w.shape)
    w_out_b = _to_bf16(ret_w_out)
    w_down_b = _to_bf16(ffn_w_down, nfp)
    conv_b = ffn_conv_b.reshape(depth, 1, 2 * nf)

    cond = jnp.concatenate([c_ctx[None, :], c, jnp.zeros((N_COND_ROWS - 1 - db, d), F32)], axis=0)
    mod = _modulation(cond, mod_w, mod_b)

    x = jnp.concatenate([x_prompt.reshape(g.tp, d), x_sample.reshape(g.ts, d)], axis=0)
    rope_tables = _rope_tables(ds, hk)
    state_shape = (b, n_ret, 2, n_heads, hk, hv)
    state_buf = None

    for l in range(depth):
        r = l // N_MIXERS
        if l % N_MIXERS == 0:
            h = _norm_mod(g, x, norm_g, mod, l, 0, F32)
            x = _pool_mixer(g, h, pool_w_b, pool_scale, x, mod, l, r)
        else:
            h = _norm_mod(g, x, norm_g, mod, l, 0, BF16)
            proj = _matmul(h, ret_w_in, r, BF16, f"ret_in_l{l}")
            o_prompt, state_buf = _retention(proj, ret_decay_ls, r, n_heads, 0, b, s,
                                             state_shape=state_shape, state_buf=state_buf)
            o_latent, _ = _retention(proj, ret_decay_ls, r, n_heads, g.tp, db, ds,
                                     rope_tables=rope_tables, s0=state_ret)
            x = _matmul_residual(g, (o_prompt, o_latent), w_out_b, r, x, mod, l, 2, f"ret_out_l{l}")
        h = _norm_mod(g, x, norm_g, mod, l, 1, BF16)
        act = _ffn_up(g, h, ffn_w_up, ffn_conv_w, conv_b, l, nfp)
        x = _matmul_residual(g, act, w_down_b, l, x, mod, l, 5, f"ffn_down_l{l}")

    y_prompt = _final_norm(g, x, final_norm_g, 0, g.tp).reshape(b, s, d)
    y_sample = _final_norm(g, x, final_norm_g, g.tp, g.ts).reshape(db, ds, d)
    return y_prompt, y_sample, state_buf
```

```python
import functools
from typing import NamedTuple

import jax
import jax.numpy as jnp
from jax import lax
from jax.experimental import pallas as pl
from jax.experimental.pallas import tpu as pltpu

F32 = jnp.float32
BF16 = jnp.bfloat16

EPS = 1e-6
CHUNK = 256
GRID_W = 64
ROPE_BASE = 10000.0
POOL_WINDOWS = (2, 4, 8, 16)
N_MIXERS = 2

V7X_LANES = 128
V7X_VMEM_BYTES = 64 * 1024 * 1024
VMEM_LIMIT = V7X_VMEM_BYTES - 8 * 1024 * 1024
N_COND_ROWS = 16
ROW_TILE = 512
MM_TILE = 1024
FF_ALIGN = 1024
FF_UP_TN = 256
FF_UP_BM = 3072
N_K_STEPS = 4
RET_HEADS_PER_STEP = 2
POOL_BLOCK = 128
CAST_ROWS = 256


class Geom(NamedTuple):
    tp: int
    ts: int
    s: int
    ds: int
    d: int
    bseq: int

    @property
    def t(self):
        return self.tp + self.ts


def _cond_index(g: Geom, i, bm):
    return jnp.maximum(i * bm - g.tp + g.ds, 0) // g.ds


def _mod_spec(g: Geom, l, which, bm, bn=None, col_axis=None):
    bn = g.d if bn is None else bn

    def index(*ids):
        j = 0 if col_axis is None else ids[col_axis]
        return (l, which, _cond_index(g, ids[0], bm), 0, j)

    return pl.BlockSpec((None, None, None, 1, bn), index)


def _params(n_axes, vmem_limit=VMEM_LIMIT):
    return pltpu.CompilerParams(dimension_semantics=("arbitrary",) * n_axes,
                                vmem_limit_bytes=vmem_limit)


def _pick_tile(n, cap, align=V7X_LANES):
    best = None
    for t in range(align, min(n, cap) + 1, align):
        if n % t == 0:
            best = t
    assert best is not None, (n, cap)
    return best


def _rem_static(x, n):
    return x & (n - 1) if n & (n - 1) == 0 else lax.rem(x, n)


def _silu(x):
    return x * jax.nn.sigmoid(x)


def _mod_kernel(c_ref, w_ref, b_ref, o_ref):
    a = _silu(c_ref[...]).astype(BF16)
    acc = jnp.dot(a, w_ref[...].astype(BF16), preferred_element_type=F32)
    o_ref[...] = acc + b_ref[...]


def _modulation(cond, mod_w, mod_b):
    depth, d, n6 = mod_w.shape
    tn = _pick_tile(n6, 512)
    out = pl.pallas_call(
        _mod_kernel,
        grid=(depth, n6 // tn),
        in_specs=[
            pl.BlockSpec((N_COND_ROWS, d), lambda l, j: (0, 0)),
            pl.BlockSpec((None, d, tn), lambda l, j: (l, 0, j)),
            pl.BlockSpec((None, 1, tn), lambda l, j: (l, 0, j)),
        ],
        out_specs=pl.BlockSpec((None, N_COND_ROWS, tn), lambda l, j: (l, 0, j)),
        out_shape=jax.ShapeDtypeStruct((depth, N_COND_ROWS, n6), F32),
        compiler_params=_params(2),
        name="adaln_modulation",
    )(cond, mod_w, mod_b.reshape(depth, 1, n6))
    out = out.reshape(depth, N_COND_ROWS, 6, 1, d)
    return jnp.transpose(out, (0, 2, 1, 3, 4))


def _to_bf16_kernel(w_ref, o_ref, *, n_real):
    i = pl.program_id(1)

    @pl.when(i < n_real)
    def _():
        o_ref[...] = w_ref[...].astype(BF16)

    @pl.when(i >= n_real)
    def _():
        o_ref[...] = jnp.zeros_like(o_ref)


def _to_bf16(w, rows_padded=None):
    nl, k, n = w.shape
    kp = k if rows_padded is None else rows_padded
    tk = _pick_tile(k, CAST_ROWS, align=16)
    assert kp % tk == 0
    n_real = k // tk
    return pl.pallas_call(
        functools.partial(_to_bf16_kernel, n_real=n_real),
        grid=(nl, kp // tk),
        in_specs=[pl.BlockSpec((None, tk, n), lambda l, i: (l, jnp.minimum(i, n_real - 1), 0))],
        out_specs=pl.BlockSpec((None, tk, n), lambda l, i: (l, i, 0)),
        out_shape=jax.ShapeDtypeStruct((nl, kp, n), BF16),
        compiler_params=_params(2),
        name=f"to_bf16_{nl}x{k}x{n}",
    )(w)


def _row_specs(g: Geom, a, block, minor=None, n_minor=1):
    def m(ids):
        return 0 if minor is None else minor(*ids)

    if not isinstance(a, tuple):
        return [pl.BlockSpec(block, lambda *ids: (ids[0], m(ids)))], [a]
    npt = g.tp // block[0]
    return [
        pl.BlockSpec(block, lambda *ids: (jnp.minimum(ids[0], npt - 1), jnp.where(ids[0] < npt, m(ids), n_minor - 1))),
        pl.BlockSpec(block, lambda *ids: (jnp.maximum(ids[0] - npt, 0), jnp.where(ids[0] < npt, 0, m(ids)))),
    ], list(a)


def _row_tile(g: Geom, refs):
    if len(refs) == 1:
        return refs[0][...]
    return jnp.where(pl.program_id(0) < g.tp // refs[0].shape[0], refs[0][...], refs[1][...])


def _norm_mod_kernel(*refs, g: Geom, j):
    *x_refs, g_ref, sc_ref, sh_ref, o_ref = refs
    x = _row_tile(g, x_refs)
    y = x * lax.rsqrt(jnp.mean(x * x, axis=-1, keepdims=True) + EPS)
    y = y * g_ref[j:j + 1, :]
    o_ref[...] = (y * (1.0 + sc_ref[...]) + sh_ref[...]).astype(o_ref.dtype)


def _norm_mod(g: Geom, x, norm_g, mod, l, j, out_dtype):
    bm = ROW_TILE // 2 if isinstance(x, tuple) else ROW_TILE
    x_specs, x_args = _row_specs(g, x, (bm, g.d))
    return pl.pallas_call(
        functools.partial(_norm_mod_kernel, g=g, j=j),
        grid=(g.t // bm,),
        in_specs=x_specs + [
            pl.BlockSpec((None, 2, g.d), lambda i: (l, 0, 0)),
            _mod_spec(g, l, 3 * j + 1, bm),
            _mod_spec(g, l, 3 * j, bm),
        ],
        out_specs=pl.BlockSpec((bm, g.d), lambda i: (i, 0)),
        out_shape=jax.ShapeDtypeStruct((g.t, g.d), out_dtype),
        compiler_params=_params(1),
        name=f"norm_mod_l{l}_{j}",
    )(*x_args, norm_g, mod, mod)


def _final_norm_kernel(x_ref, g_ref, o_ref):
    x = x_ref[...]
    y = x * lax.rsqrt(jnp.mean(x * x, axis=-1, keepdims=True) + EPS)
    o_ref[...] = y * g_ref[...]


def _final_norm(g: Geom, x, final_g, row0, rows):
    bm = ROW_TILE
    blk0 = row0 // bm
    return pl.pallas_call(
        _final_norm_kernel,
        grid=(rows // bm,),
        in_specs=[
            pl.BlockSpec((bm, g.d), lambda i: (blk0 + i, 0)),
            pl.BlockSpec((1, g.d), lambda i: (0, 0)),
        ],
        out_specs=pl.BlockSpec((bm, g.d), lambda i: (i, 0)),
        out_shape=jax.ShapeDtypeStruct((rows, g.d), F32),
        compiler_params=_params(1),
        name=f"final_norm_{row0}",
    )(x, final_g.reshape(1, g.d))


def _mm_plain_kernel(a_ref, b_ref, o_ref):
    b = b_ref[...].astype(BF16)
    o_ref[...] = jnp.dot(a_ref[...], b, preferred_element_type=F32).astype(o_ref.dtype)


def _matmul(a, w, r, out_dtype, name):
    t, k = a.shape
    n = w.shape[-1]
    bm = _pick_tile(t, 2 * MM_TILE)
    bn = _pick_tile(n, MM_TILE // 2)
    return pl.pallas_call(
        _mm_plain_kernel,
        grid=(t // bm, n // bn),
        in_specs=[
            pl.BlockSpec((bm, k), lambda i, j: (i, 0)),
            pl.BlockSpec((None, k, bn), lambda i, j: (r, 0, j)),
        ],
        out_specs=pl.BlockSpec((bm, bn), lambda i, j: (i, j)),
        out_shape=jax.ShapeDtypeStruct((t, n), out_dtype),
        compiler_params=_params(2),
        name=name,
    )(a, w)


def _mm_residual_kernel(*refs, g: Geom, nk):
    *a_refs, b_ref, x_ref, g_ref, o_ref, acc_ref = refs
    k = pl.program_id(2)

    @pl.when((pl.program_id(0) == 0) & (pl.program_id(1) == 0) & (k == 0))
    def _():
        acc_ref[...] = jnp.zeros_like(acc_ref)

    acc = acc_ref[...] + jnp.dot(_row_tile(g, a_refs), b_ref[...], preferred_element_type=F32)
    acc_ref[...] = jnp.where(k == nk - 1, 0.0, acc)
    o_ref[...] = x_ref[...] + g_ref[...] * acc


def _matmul_residual(g: Geom, a, w, r, x, mod, l, gate, name):
    t, n = x.shape
    bm = _pick_tile(g.bseq, MM_TILE)
    bn = _pick_tile(n, MM_TILE)
    nk = N_K_STEPS
    k = w.shape[-2]
    bk = k // nk
    assert k % nk == 0 and bk % V7X_LANES == 0, (k, nk)
    a_specs, a_args = _row_specs(g, a, (bm, bk), minor=lambda i, j, kk: kk, n_minor=nk)
    return pl.pallas_call(
        functools.partial(_mm_residual_kernel, g=g, nk=nk),
        grid=(t // bm, n // bn, nk),
        in_specs=a_specs + [
            pl.BlockSpec((None, bk, bn), lambda i, j, kk: (r, kk, j)),
            pl.BlockSpec((bm, bn), lambda i, j, kk: (i, j)),
            _mod_spec(g, l, gate, bm, bn, col_axis=1),
        ],
        out_specs=pl.BlockSpec((bm, bn), lambda i, j, kk: (i, j)),
        out_shape=jax.ShapeDtypeStruct((t, n), F32),
        scratch_shapes=[pltpu.VMEM((bm, bn), F32)],
        compiler_params=_params(3),
        name=name,
    )(*a_args, w, x, mod)


def _seq_position(g: Geom, bm, r0=0):
    is_prompt = pl.program_id(0) * bm + r0 < g.tp
    t = lax.broadcasted_iota(jnp.int32, (g.bseq, 1), 0)
    pos = jnp.where(is_prompt, _rem_static(t, g.s), _rem_static(t, g.ds))
    length = jnp.where(is_prompt, g.s, g.ds)
    return pos, length


def _ffn_up_kernel(h_ref, wa_ref, wg_ref, cwa_ref, cwg_ref, cba_ref, cbg_ref, o_ref, *, g: Geom, n_real):
    bm = h_ref.shape[0]
    j = pl.program_id(1)

    @pl.when(j >= n_real)
    def _():
        o_ref[...] = jnp.zeros_like(o_ref)

    @pl.when(j < n_real)
    def _():
        wa = wa_ref[...].astype(BF16)
        wg = wg_ref[...].astype(BF16)

        for r0 in range(0, bm, g.bseq):
            h = h_ref[r0:r0 + g.bseq, :]
            pos, length = _seq_position(g, bm, r0)
            first = pos == 0
            last = pos == length - 1

            def conv(w, cw_ref, cb_ref):
                u = jnp.dot(h, w, preferred_element_type=F32)
                prev = jnp.where(first, 0.0, pltpu.roll(u, 1, axis=0))
                nxt = jnp.where(last, 0.0, pltpu.roll(u, g.bseq - 1, axis=0))
                return prev * cw_ref[0:1, :] + u * cw_ref[1:2, :] + nxt * cw_ref[2:3, :] + cb_ref[...]

            a = conv(wa, cwa_ref, cba_ref)
            gt = conv(wg, cwg_ref, cbg_ref)
            o_ref[r0:r0 + g.bseq, :] = (_silu(gt) * a).astype(o_ref.dtype)


def _ffn_up(g: Geom, h, w_up, conv_w, conv_b, l, nfp):
    nf = w_up.shape[-1] // 2
    tn = _pick_tile(nf, FF_UP_TN)
    assert nfp % tn == 0
    n_real = nf // tn
    bm = max(m * g.bseq for m in range(1, FF_UP_BM // g.bseq + 1) if g.t % (m * g.bseq) == 0)

    def col(j):
        return jnp.minimum(j, n_real - 1)

    return pl.pallas_call(
        functools.partial(_ffn_up_kernel, g=g, n_real=n_real),
        grid=(g.t // bm, nfp // tn),
        in_specs=[
            pl.BlockSpec((bm, g.d), lambda i, j: (i, 0), pipeline_mode=pl.Buffered(1)),
            pl.BlockSpec((None, g.d, tn), lambda i, j: (l, 0, col(j))),
            pl.BlockSpec((None, g.d, tn), lambda i, j: (l, 0, n_real + col(j))),
            pl.BlockSpec((None, 3, tn), lambda i, j: (l, 0, col(j))),
            pl.BlockSpec((None, 3, tn), lambda i, j: (l, 0, n_real + col(j))),
            pl.BlockSpec((None, 1, tn), lambda i, j: (l, 0, col(j))),
            pl.BlockSpec((None, 1, tn), lambda i, j: (l, 0, n_real + col(j))),
        ],
        out_specs=pl.BlockSpec((bm, tn), lambda i, j: (i, j)),
        out_shape=jax.ShapeDtypeStruct((g.t, nfp), BF16),
        compiler_params=_params(2),
        name=f"ffn_up_l{l}",
    )(h, w_up, w_up, conv_w, conv_w, conv_b, conv_b)


def _pool_kernel(h_ref, w_ref, ps_ref, *refs, g: Geom):
    *x_refs, g_ref, o_ref = refs
    bm = h_ref.shape[0]
    gi = pl.program_id(1)
    half = jnp.int32(0)
    for k, win in enumerate(POOL_WINDOWS):
        half = jnp.where(gi == k, win // 2, half)

    pos, length = _seq_position(g, bm)
    cnt = (jnp.minimum(pos + half, length) - jnp.maximum(pos - half, 0)).astype(F32)

    h = h_ref[...]
    h_hi = h.astype(BF16)
    h_lo = (h - h_hi.astype(F32)).astype(BF16)

    nb = bm // POOL_BLOCK
    sums = []
    for rb in range(nb):
        r0, c0, c1 = rb * POOL_BLOCK, max(rb - 1, 0) * POOL_BLOCK, min(rb + 2, nb) * POOL_BLOCK
        off = ((c0 - r0) + lax.broadcasted_iota(jnp.int32, (POOL_BLOCK, c1 - c0), 1)
               - lax.broadcasted_iota(jnp.int32, (POOL_BLOCK, c1 - c0), 0))
        src = pos[r0:r0 + POOL_BLOCK] + off
        band = (off >= -half) & (off < half) & (src >= 0) & (src < length)
        band = jnp.where(band, 1.0, 0.0).astype(BF16)
        sums.append(jnp.dot(band, h_hi[c0:c1], preferred_element_type=F32)
                    + jnp.dot(band, h_lo[c0:c1], preferred_element_type=F32))
    wsum = jnp.concatenate(sums, axis=0)
    pooled = (wsum / cnt - h).astype(BF16)
    y = jnp.dot(pooled, w_ref[...], preferred_element_type=F32) * ps_ref[...]
    o_ref[...] = _row_tile(g, x_refs) + g_ref[...] * y


def _pool_mixer(g: Geom, h, pool_w, pool_scale, x, mod, l, r):
    ng, pg = pool_w.shape[1], pool_w.shape[2]
    assert ng == len(POOL_WINDOWS) and pg % V7X_LANES == 0
    assert max(POOL_WINDOWS) // 2 <= POOL_BLOCK and g.bseq % POOL_BLOCK == 0
    bm = g.bseq
    x_specs, x_args = _row_specs(g, x, (bm, pg), minor=lambda i, gi: gi, n_minor=ng)
    return pl.pallas_call(
        functools.partial(_pool_kernel, g=g),
        grid=(g.t // bm, ng),
        in_specs=[
            pl.BlockSpec((bm, pg), lambda i, gi: (i, gi)),
            pl.BlockSpec((None, None, pg, pg), lambda i, gi: (r, gi, 0, 0)),
            pl.BlockSpec((None, 1, pg), lambda i, gi: (r, 0, gi)),
        ] + x_specs + [_mod_spec(g, l, 2, bm, pg, col_axis=1)],
        out_specs=pl.BlockSpec((bm, pg), lambda i, gi: (i, gi)),
        out_shape=jax.ShapeDtypeStruct((g.t, g.d), F32),
        compiler_params=_params(2),
        name=f"pool_mixer_l{l}",
    )(h, pool_w, pool_scale.reshape(pool_scale.shape[0], 1, g.d), *x_args, mod)


def _ret_kernel(*refs, r, n_heads, seq, rope, has_s0, emit_state, fill_state, n_aliased, hps):
    it = iter(refs)
    dls_ref = next(it)
    q_ref, k_ref, v_ref, gate_ref = next(it), next(it), next(it), next(it)
    cos_ref, sin_ref = (next(it), next(it)) if rope else (None, None)
    s0_ref = next(it) if has_s0 else None
    for _ in range(n_aliased):
        next(it)
    o_ref = next(it)
    st_ref = next(it) if emit_state else None
    oacc_ref = next(it)

    hk = q_ref.shape[1] // hps
    hv = v_ref.shape[1] // hps
    hh = hk // 2
    n_chunks = seq // CHUNK

    if fill_state:
        for layer in range(st_ref.shape[0]):
            if layer != r:
                st_ref[layer] = jnp.zeros(st_ref.shape[1:], F32)
        st_ref = st_ref.at[r]

    ri = lax.broadcasted_iota(jnp.int32, (CHUNK, CHUNK), 0).astype(F32)
    ci = lax.broadcasted_iota(jnp.int32, (CHUNK, CHUNK), 1).astype(F32)
    diff = ri - ci
    idx = lax.broadcasted_iota(jnp.int32, (CHUNK, 1), 0).astype(F32)

    def rot(x, rows):
        parts = []
        for p in range(2):
            xp = x[:, p * hh:(p + 1) * hh]
            parts.append(xp * cos_ref[rows, p * hh:(p + 1) * hh]
                         + pltpu.roll(xp, hh // 2, axis=1) * sin_ref[rows, p * hh:(p + 1) * hh])
        return jnp.concatenate(parts, axis=1)

    def add_cross(o, qc, s, qdec):
        if s is None:
            return o
        return o + jnp.dot(qc, s.astype(BF16), preferred_element_type=F32) * qdec

    def advance(s, kf, kdec, cdec, vc):
        kd = (kf * kdec).astype(BF16)
        kv = lax.dot_general(kd, vc, (((0,), (0,)), ((), ())), preferred_element_type=F32)
        return kv if s is None else s * cdec + kv

    for hd in range(hps):
        head = pl.program_id(1) * hps + hd
        kcol = slice(hd * hk, (hd + 1) * hk)
        vcol = slice(hd * hv, (hd + 1) * hv)

        def log_gamma(direction):
            ls = dls_ref[(r * 2 + direction) * n_heads + head]
            return -jnp.exp(jnp.full((1, 1), ls, F32))

        lg_f, lg_b = log_gamma(0), log_gamma(1)
        intra = (jnp.where(diff >= 0, jnp.exp(lg_f * jnp.maximum(diff, 0.0)), 0.0)
                 + jnp.where(diff <= 0, jnp.exp(lg_b * jnp.maximum(-diff, 0.0)), 0.0))
        qdec_f = jnp.exp(lg_f * (idx + 1.0))
        kdec_f = jnp.exp(lg_f * (CHUNK - 1.0 - idx))
        qdec_b = jnp.exp(lg_b * (CHUNK - idx))
        kdec_b = jnp.exp(lg_b * idx)
        cdec_f = jnp.exp(lg_f * CHUNK)
        cdec_b = jnp.exp(lg_b * CHUNK)

        qs, kfs = [], []
        for c in range(n_chunks):
            rows = slice(c * CHUNK, (c + 1) * CHUNK)
            q = q_ref[rows, kcol]
            kf = k_ref[rows, kcol].astype(F32) * (hk ** -0.5)
            if rope:
                q = rot(q.astype(F32), rows).astype(BF16)
                kf = rot(kf, rows)
            qs.append(q)
            kfs.append(kf)

        s = s0_ref[0, hd] if has_s0 else None
        for c in range(n_chunks):
            rows = slice(c * CHUNK, (c + 1) * CHUNK)
            vc = v_ref[rows, vcol]
            scores = lax.dot_general(qs[c], kfs[c].astype(BF16), (((1,), (1,)), ((), ())),
                                     preferred_element_type=F32)
            o = jnp.dot((scores * intra).astype(BF16), vc, preferred_element_type=F32)
            oacc_ref[rows, vcol] = add_cross(o, qs[c], s, qdec_f)
            if emit_state or c + 1 < n_chunks:
                s = advance(s, kfs[c], kdec_f, cdec_f, vc)
        if emit_state:
            st_ref[0, hd] = s

        s = s0_ref[1, hd] if has_s0 else None
        for c in reversed(range(n_chunks)):
            rows = slice(c * CHUNK, (c + 1) * CHUNK)
            vc = v_ref[rows, vcol]
            o = add_cross(oacc_ref[rows, vcol], qs[c], s, qdec_b)
            mu = jnp.mean(o, axis=-1, keepdims=True)
            dev = o - mu
            var = jnp.mean(dev * dev, axis=-1, keepdims=True)
            normed = dev * lax.rsqrt(var + EPS)
            gate = gate_ref[rows, vcol].astype(F32)
            o_ref[rows, vcol] = (_silu(gate) * normed).astype(o_ref.dtype)
            if emit_state or c > 0:
                s = advance(s, kfs[c], kdec_b, cdec_b, vc)
        if emit_state:
            st_ref[1, hd] = s


def _retention(proj, decay_ls, r, n_heads, row0, n_seq, seq, rope_tables=None, s0=None,
               state_shape=None, state_buf=None):
    t, ret_in = proj.shape
    hk = ret_in // (6 * n_heads)
    hv = 2 * hk
    assert seq % CHUNK == 0 and row0 % seq == 0
    rb0 = row0 // seq
    rope = rope_tables is not None
    has_s0 = s0 is not None
    emit_state = state_shape is not None
    fill_state = emit_state and state_buf is None

    hps = RET_HEADS_PER_STEP
    assert n_heads % hps == 0
    nhb = n_heads // hps
    state_spec = pl.BlockSpec((None, None, 2, hps, hk, hv), lambda b, h, *_: (b, r, 0, h, 0, 0))

    in_specs = [
        pl.BlockSpec((seq, hps * hk), lambda b, h, *_: (rb0 + b, h)),
        pl.BlockSpec((seq, hps * hk), lambda b, h, *_: (rb0 + b, nhb + h)),
        pl.BlockSpec((seq, hps * hv), lambda b, h, *_: (rb0 + b, nhb + h)),
        pl.BlockSpec((seq, hps * hv), lambda b, h, *_: (rb0 + b, 2 * nhb + h)),
    ]
    args = [proj, proj, proj, proj]
    if rope:
        in_specs += [pl.BlockSpec((seq, hk), lambda b, h, *_: (0, 0))] * 2
        args += list(rope_tables)
    if has_s0:
        in_specs.append(state_spec)
        args.append(s0)
    aliases = {}
    out_specs = [pl.BlockSpec((seq, hps * hv), lambda b, h, *_: (b, h))]
    out_shape = [jax.ShapeDtypeStruct((n_seq * seq, n_heads * hv), BF16)]
    if emit_state:
        if fill_state:
            out_specs.append(pl.BlockSpec((None, state_shape[1], 2, hps, hk, hv),
                                          lambda b, h, *_: (b, 0, 0, h, 0, 0)))
        else:
            out_specs.append(state_spec)
            aliases[1 + len(args)] = 1
            in_specs.append(pl.BlockSpec(memory_space=pl.ANY))
            args.append(state_buf)
        out_shape.append(jax.ShapeDtypeStruct(state_shape, F32))

    out = pl.pallas_call(
        functools.partial(_ret_kernel, r=r, n_heads=n_heads, seq=seq, rope=rope, has_s0=has_s0,
                          emit_state=emit_state, fill_state=fill_state, n_aliased=len(aliases), hps=hps),
        grid_spec=pltpu.PrefetchScalarGridSpec(
            num_scalar_prefetch=1,
            grid=(n_seq, nhb),
            in_specs=in_specs,
            out_specs=out_specs,
            scratch_shapes=[pltpu.VMEM((seq, hps * hv), F32)],
        ),
        out_shape=out_shape,
        input_output_aliases=aliases,
        compiler_params=_params(2),
        name=f"retention_r{r}_{'latent' if rope else 'prompt'}",
    )(decay_ls.reshape(-1), *args)
    return out if emit_state else (out[0], None)


def _rope_tables(seq, hk):
    rows = seq // GRID_W
    row_ids = jnp.repeat(jnp.arange(rows), GRID_W).astype(F32)
    col_ids = jnp.tile(jnp.arange(GRID_W), rows).astype(F32)
    n = hk // 4
    freqs = ROPE_BASE ** (-jnp.arange(n, dtype=F32) / n)
    cos, sin = [], []
    for ang in (row_ids[:, None] * freqs, col_ids[:, None] * freqs):
        c, s = jnp.cos(ang), jnp.sin(ang)
        cos += [c, c]
        sin += [-s, s]
    return jnp.concatenate(cos, axis=1), jnp.concatenate(sin, axis=1)


def kernel(x_prompt, x_sample, state_ret, c, c_ctx, mod_w, mod_b, norm_g, final_norm_g, pool_w, pool_scale,
           ret_w_in, ret_w_out, ret_decay_ls, ffn_w_up, ffn_conv_w, ffn_conv_b, ffn_w_down):
    b, s, d = x_prompt.shape
    db, ds, _ = x_sample.shape
    depth = mod_w.shape[0]
    n_heads = ret_decay_ls.shape[-1]
    n_ret = ret_w_in.shape[0]
    hk = d // n_heads
    hv = ret_w_out.shape[1] // n_heads
    nf = ffn_w_down.shape[1]
    nfp = -(-nf // FF_ALIGN) * FF_ALIGN

    bseq = max(s, ds)
    g = Geom(tp=b * s, ts=db * ds, s=s, ds=ds, d=d, bseq=bseq)
    assert bseq % s == 0 and bseq % ds == 0 and g.tp % bseq == 0 and g.ts % bseq == 0
    assert g.tp % ds == 0 and g.tp % ROW_TILE == 0 and ds % ROW_TILE == 0
    assert 1 + db <= N_COND_ROWS and hv == 2 * hk and ds % GRID_W == 0

    pool_w_b = _to_bf16(pool_w.reshape((-1,) + pool_w.shape[2:])).reshape(pool_w.shape)
    w_out_b = _to_bf16(ret_w_out)
    w_down_b = _to_bf16(ffn_w_down, nfp)
    conv_b = ffn_conv_b.reshape(depth, 1, 2 * nf)

    cond = jnp.concatenate([c_ctx[None, :], c, jnp.zeros((N_COND_ROWS - 1 - db, d), F32)], axis=0)
    mod = _modulation(cond, mod_w, mod_b)

    x = (x_prompt.reshape(g.tp, d), x_sample.reshape(g.ts, d))
    rope_tables = _rope_tables(ds, hk)
    state_shape = (b, n_ret, 2, n_heads, hk, hv)
    state_buf = None

    for l in range(depth):
        r = l // N_MIXERS
        if l % N_MIXERS == 0:
            h = _norm_mod(g, x, norm_g, mod, l, 0, F32)
            x = _pool_mixer(g, h, pool_w_b, pool_scale, x, mod, l, r)
        else:
            h = _norm_mod(g, x, norm_g, mod, l, 0, BF16)
            proj = _matmul(h, ret_w_in, r, BF16, f"ret_in_l{l}")
            o_prompt, state_buf = _retention(proj, ret_decay_ls, r, n_heads, 0, b, s,
                                             state_shape=state_shape, state_buf=state_buf)
            o_latent, _ = _retention(proj, ret_decay_ls, r, n_heads, g.tp, db, ds,
                                     rope_tables=rope_tables, s0=state_ret)
            x = _matmul_residual(g, (o_prompt, o_latent), w_out_b, r, x, mod, l, 2, f"ret_out_l{l}")
        h = _norm_mod(g, x, norm_g, mod, l, 1, BF16)
        act = _ffn_up(g, h, ffn_w_up, ffn_conv_w, conv_b, l, nfp)
        x = _matmul_residual(g, act, w_down_b, l, x, mod, l, 5, f"ffn_down_l{l}")

    y_prompt = _final_norm(g, x, final_norm_g, 0, g.tp).reshape(b, s, d)
    y_sample = _final_norm(g, x, final_norm_g, g.tp, g.ts).reshape(db, ds, d)
    return y_prompt, y_sample, state_buf
```

```python
import functools
from typing import NamedTuple

import jax
import jax.numpy as jnp
from jax import lax
from jax.experimental import pallas as pl
from jax.experimental.pallas import tpu as pltpu

F32 = jnp.float32
BF16 = jnp.bfloat16

EPS = 1e-6
CHUNK = 256
GRID_W = 64
ROPE_BASE = 10000.0
POOL_WINDOWS = (2, 4, 8, 16)
N_MIXERS = 2

V7X_LANES = 128
V7X_VMEM_BYTES = 64 * 1024 * 1024
VMEM_LIMIT = V7X_VMEM_BYTES - 8 * 1024 * 1024
N_COND_ROWS = 16
ROW_TILE = 512
MM_TILE = 1024
FF_ALIGN = 1024
FF_UP_TN = 256
FF_UP_BM = 4096
N_K_STEPS = 4
RET_HEADS_PER_STEP = 2
POOL_BLOCK = 128
CAST_ROWS = 256


class Geom(NamedTuple):
    tp: int
    ts: int
    s: int
    ds: int
    d: int
    bseq: int

    @property
    def t(self):
        return self.tp + self.ts


def _cond_index(g: Geom, i, bm):
    return jnp.maximum(i * bm - g.tp + g.ds, 0) // g.ds


def _mod_spec(g: Geom, l, which, bm, bn=None, col_axis=None):
    bn = g.d if bn is None else bn

    def index(*ids):
        j = 0 if col_axis is None else ids[col_axis]
        return (l, which, _cond_index(g, ids[0], bm), 0, j)

    return pl.BlockSpec((None, None, None, 1, bn), index)


def _params(n_axes, vmem_limit=VMEM_LIMIT):
    return pltpu.CompilerParams(dimension_semantics=("arbitrary",) * n_axes,
                                vmem_limit_bytes=vmem_limit)


def _pick_tile(n, cap, align=V7X_LANES):
    best = None
    for t in range(align, min(n, cap) + 1, align):
        if n % t == 0:
            best = t
    assert best is not None, (n, cap)
    return best


def _rem_static(x, n):
    return x & (n - 1) if n & (n - 1) == 0 else lax.rem(x, n)


def _silu(x):
    return x * jax.nn.sigmoid(x)


def _mod_kernel(c_ref, w_ref, b_ref, o_ref):
    a = _silu(c_ref[...]).astype(BF16)
    acc = jnp.dot(a, w_ref[...].astype(BF16), preferred_element_type=F32)
    o_ref[...] = acc + b_ref[...]


def _modulation(cond, mod_w, mod_b):
    depth, d, n6 = mod_w.shape
    tn = _pick_tile(n6, 512)
    out = pl.pallas_call(
        _mod_kernel,
        grid=(depth, n6 // tn),
        in_specs=[
            pl.BlockSpec((N_COND_ROWS, d), lambda l, j: (0, 0)),
            pl.BlockSpec((None, d, tn), lambda l, j: (l, 0, j)),
            pl.BlockSpec((None, 1, tn), lambda l, j: (l, 0, j)),
        ],
        out_specs=pl.BlockSpec((None, N_COND_ROWS, tn), lambda l, j: (l, 0, j)),
        out_shape=jax.ShapeDtypeStruct((depth, N_COND_ROWS, n6), F32),
        compiler_params=_params(2),
        name="adaln_modulation",
    )(cond, mod_w, mod_b.reshape(depth, 1, n6))
    out = out.reshape(depth, N_COND_ROWS, 6, 1, d)
    return jnp.transpose(out, (0, 2, 1, 3, 4))


def _to_bf16_kernel(w_ref, o_ref, *, n_real):
    i = pl.program_id(1)

    @pl.when(i < n_real)
    def _():
        o_ref[...] = w_ref[...].astype(BF16)

    @pl.when(i >= n_real)
    def _():
        o_ref[...] = jnp.zeros_like(o_ref)


def _to_bf16(w, rows_padded=None):
    nl, k, n = w.shape
    kp = k if rows_padded is None else rows_padded
    tk = _pick_tile(k, CAST_ROWS, align=16)
    assert kp % tk == 0
    n_real = k // tk
    return pl.pallas_call(
        functools.partial(_to_bf16_kernel, n_real=n_real),
        grid=(nl, kp // tk),
        in_specs=[pl.BlockSpec((None, tk, n), lambda l, i: (l, jnp.minimum(i, n_real - 1), 0))],
        out_specs=pl.BlockSpec((None, tk, n), lambda l, i: (l, i, 0)),
        out_shape=jax.ShapeDtypeStruct((nl, kp, n), BF16),
        compiler_params=_params(2),
        name=f"to_bf16_{nl}x{k}x{n}",
    )(w)


def _row_specs(g: Geom, a, block, minor=None, n_minor=1):
    def m(ids):
        return 0 if minor is None else minor(*ids)

    if not isinstance(a, tuple):
        return [pl.BlockSpec(block, lambda *ids: (ids[0], m(ids)))], [a]
    npt = g.tp // block[0]
    return [
        pl.BlockSpec(block, lambda *ids: (jnp.minimum(ids[0], npt - 1), jnp.where(ids[0] < npt, m(ids), n_minor - 1))),
        pl.BlockSpec(block, lambda *ids: (jnp.maximum(ids[0] - npt, 0), jnp.where(ids[0] < npt, 0, m(ids)))),
    ], list(a)


def _row_tile(g: Geom, refs):
    if len(refs) == 1:
        return refs[0][...]
    return jnp.where(pl.program_id(0) < g.tp // refs[0].shape[0], refs[0][...], refs[1][...])


def _norm_mod_kernel(*refs, g: Geom, j):
    *x_refs, g_ref, sc_ref, sh_ref, o_ref = refs
    x = _row_tile(g, x_refs)
    y = x * lax.rsqrt(jnp.mean(x * x, axis=-1, keepdims=True) + EPS)
    y = y * g_ref[j:j + 1, :]
    o_ref[...] = (y * (1.0 + sc_ref[...]) + sh_ref[...]).astype(o_ref.dtype)


def _norm_mod(g: Geom, x, norm_g, mod, l, j, out_dtype):
    bm = ROW_TILE // 2 if isinstance(x, tuple) else ROW_TILE
    x_specs, x_args = _row_specs(g, x, (bm, g.d))
    return pl.pallas_call(
        functools.partial(_norm_mod_kernel, g=g, j=j),
        grid=(g.t // bm,),
        in_specs=x_specs + [
            pl.BlockSpec((None, 2, g.d), lambda i: (l, 0, 0)),
            _mod_spec(g, l, 3 * j + 1, bm),
            _mod_spec(g, l, 3 * j, bm),
        ],
        out_specs=pl.BlockSpec((bm, g.d), lambda i: (i, 0)),
        out_shape=jax.ShapeDtypeStruct((g.t, g.d), out_dtype),
        compiler_params=_params(1),
        name=f"norm_mod_l{l}_{j}",
    )(*x_args, norm_g, mod, mod)


def _final_norm_kernel(x_ref, g_ref, o_ref):
    x = x_ref[...]
    y = x * lax.rsqrt(jnp.mean(x * x, axis=-1, keepdims=True) + EPS)
    o_ref[...] = y * g_ref[...]


def _final_norm(g: Geom, x, final_g, row0, rows):
    bm = ROW_TILE
    blk0 = row0 // bm
    return pl.pallas_call(
        _final_norm_kernel,
        grid=(rows // bm,),
        in_specs=[
            pl.BlockSpec((bm, g.d), lambda i: (blk0 + i, 0)),
            pl.BlockSpec((1, g.d), lambda i: (0, 0)),
        ],
        out_specs=pl.BlockSpec((bm, g.d), lambda i: (i, 0)),
        out_shape=jax.ShapeDtypeStruct((rows, g.d), F32),
        compiler_params=_params(1),
        name=f"final_norm_{row0}",
    )(x, final_g.reshape(1, g.d))


def _mm_plain_kernel(a_ref, b_ref, o_ref):
    b = b_ref[...].astype(BF16)
    o_ref[...] = jnp.dot(a_ref[...], b, preferred_element_type=F32).astype(o_ref.dtype)


def _matmul(a, w, r, out_dtype, name):
    t, k = a.shape
    n = w.shape[-1]
    bm = _pick_tile(t, 2 * MM_TILE)
    bn = _pick_tile(n, MM_TILE // 2)
    return pl.pallas_call(
        _mm_plain_kernel,
        grid=(t // bm, n // bn),
        in_specs=[
            pl.BlockSpec((bm, k), lambda i, j: (i, 0)),
            pl.BlockSpec((None, k, bn), lambda i, j: (r, 0, j)),
        ],
        out_specs=pl.BlockSpec((bm, bn), lambda i, j: (i, j)),
        out_shape=jax.ShapeDtypeStruct((t, n), out_dtype),
        compiler_params=_params(2),
        name=name,
    )(a, w)


def _mm_residual_kernel(*refs, g: Geom, nk):
    *a_refs, b_ref, x_ref, g_ref, o_ref, acc_ref = refs
    k = pl.program_id(2)

    @pl.when((pl.program_id(0) == 0) & (pl.program_id(1) == 0) & (k == 0))
    def _():
        acc_ref[...] = jnp.zeros_like(acc_ref)

    acc = acc_ref[...] + jnp.dot(_row_tile(g, a_refs), b_ref[...], preferred_element_type=F32)
    acc_ref[...] = jnp.where(k == nk - 1, 0.0, acc)
    o_ref[...] = x_ref[...] + g_ref[...] * acc


def _matmul_residual(g: Geom, a, w, r, x, mod, l, gate, name):
    t, n = x.shape
    bm = _pick_tile(g.bseq, MM_TILE)
    bn = _pick_tile(n, MM_TILE)
    nk = N_K_STEPS
    k = w.shape[-2]
    bk = k // nk
    assert k % nk == 0 and bk % V7X_LANES == 0, (k, nk)
    a_specs, a_args = _row_specs(g, a, (bm, bk), minor=lambda i, j, kk: kk, n_minor=nk)
    return pl.pallas_call(
        functools.partial(_mm_residual_kernel, g=g, nk=nk),
        grid=(t // bm, n // bn, nk),
        in_specs=a_specs + [
            pl.BlockSpec((None, bk, bn), lambda i, j, kk: (r, kk, j)),
            pl.BlockSpec((bm, bn), lambda i, j, kk: (i, j)),
            _mod_spec(g, l, gate, bm, bn, col_axis=1),
        ],
        out_specs=pl.BlockSpec((bm, bn), lambda i, j, kk: (i, j)),
        out_shape=jax.ShapeDtypeStruct((t, n), F32),
        scratch_shapes=[pltpu.VMEM((bm, bn), F32)],
        compiler_params=_params(3),
        name=name,
    )(*a_args, w, x, mod)


def _seq_position(g: Geom, bm, r0=0):
    is_prompt = pl.program_id(0) * bm + r0 < g.tp
    t = lax.broadcasted_iota(jnp.int32, (g.bseq, 1), 0)
    pos = jnp.where(is_prompt, _rem_static(t, g.s), _rem_static(t, g.ds))
    length = jnp.where(is_prompt, g.s, g.ds)
    return pos, length


def _ffn_up_kernel(h_ref, wa_ref, wg_ref, cwa_ref, cwg_ref, cba_ref, cbg_ref, o_ref, *, g: Geom, n_real):
    bm = h_ref.shape[0]
    j = pl.program_id(1)

    @pl.when(j >= n_real)
    def _():
        o_ref[...] = jnp.zeros_like(o_ref)

    @pl.when(j < n_real)
    def _():
        wa = wa_ref[...].astype(BF16)
        wg = wg_ref[...].astype(BF16)

        for r0 in range(0, bm, g.bseq):
            h = h_ref[r0:r0 + g.bseq, :]
            pos, length = _seq_position(g, bm, r0)
            first = pos == 0
            last = pos == length - 1

            def conv(w, cw_ref, cb_ref):
                u = jnp.dot(h, w, preferred_element_type=F32)
                prev = jnp.where(first, 0.0, pltpu.roll(u, 1, axis=0))
                nxt = jnp.where(last, 0.0, pltpu.roll(u, g.bseq - 1, axis=0))
                return prev * cw_ref[0:1, :] + u * cw_ref[1:2, :] + nxt * cw_ref[2:3, :] + cb_ref[...]

            a = conv(wa, cwa_ref, cba_ref)
            gt = conv(wg, cwg_ref, cbg_ref)
            o_ref[r0:r0 + g.bseq, :] = (_silu(gt) * a).astype(o_ref.dtype)


def _ffn_up(g: Geom, h, w_up, conv_w, conv_b, l, nfp):
    nf = w_up.shape[-1] // 2
    tn = _pick_tile(nf, FF_UP_TN)
    assert nfp % tn == 0
    n_real = nf // tn
    bm = max(m * g.bseq for m in range(1, FF_UP_BM // g.bseq + 1) if g.t % (m * g.bseq) == 0)

    def col(j):
        return jnp.minimum(j, n_real - 1)

    return pl.pallas_call(
        functools.partial(_ffn_up_kernel, g=g, n_real=n_real),
        grid=(g.t // bm, nfp // tn),
        in_specs=[
            pl.BlockSpec((bm, g.d), lambda i, j: (i, 0), pipeline_mode=pl.Buffered(1)),
            pl.BlockSpec((None, g.d, tn), lambda i, j: (l, 0, col(j))),
            pl.BlockSpec((None, g.d, tn), lambda i, j: (l, 0, n_real + col(j))),
            pl.BlockSpec((None, 3, tn), lambda i, j: (l, 0, col(j))),
            pl.BlockSpec((None, 3, tn), lambda i, j: (l, 0, n_real + col(j))),
            pl.BlockSpec((None, 1, tn), lambda i, j: (l, 0, col(j))),
            pl.BlockSpec((None, 1, tn), lambda i, j: (l, 0, n_real + col(j))),
        ],
        out_specs=pl.BlockSpec((bm, tn), lambda i, j: (i, j)),
        out_shape=jax.ShapeDtypeStruct((g.t, nfp), BF16),
        compiler_params=_params(2, vmem_limit=V7X_VMEM_BYTES - 512 * 1024),
        name=f"ffn_up_l{l}",
    )(h, w_up, w_up, conv_w, conv_w, conv_b, conv_b)


def _pool_kernel(h_ref, w_ref, ps_ref, *refs, g: Geom):
    *x_refs, g_ref, o_ref = refs
    bm = h_ref.shape[0]
    gi = pl.program_id(1)
    half = jnp.int32(0)
    for k, win in enumerate(POOL_WINDOWS):
        half = jnp.where(gi == k, win // 2, half)

    pos, length = _seq_position(g, bm)
    cnt = (jnp.minimum(pos + half, length) - jnp.maximum(pos - half, 0)).astype(F32)

    h = h_ref[...]
    h_hi = h.astype(BF16)
    h_lo = (h - h_hi.astype(F32)).astype(BF16)

    nb = bm // POOL_BLOCK
    sums = []
    for rb in range(nb):
        r0, c0, c1 = rb * POOL_BLOCK, max(rb - 1, 0) * POOL_BLOCK, min(rb + 2, nb) * POOL_BLOCK
        off = ((c0 - r0) + lax.broadcasted_iota(jnp.int32, (POOL_BLOCK, c1 - c0), 1)
               - lax.broadcasted_iota(jnp.int32, (POOL_BLOCK, c1 - c0), 0))
        src = pos[r0:r0 + POOL_BLOCK] + off
        band = (off >= -half) & (off < half) & (src >= 0) & (src < length)
        band = jnp.where(band, 1.0, 0.0).astype(BF16)
        sums.append(jnp.dot(band, h_hi[c0:c1], preferred_element_type=F32)
                    + jnp.dot(band, h_lo[c0:c1], preferred_element_type=F32))
    wsum = jnp.concatenate(sums, axis=0)
    pooled = (wsum / cnt - h).astype(BF16)
    y = jnp.dot(pooled, w_ref[...], preferred_element_type=F32) * ps_ref[...]
    o_ref[...] = _row_tile(g, x_refs) + g_ref[...] * y


def _pool_mixer(g: Geom, h, pool_w, pool_scale, x, mod, l, r):
    ng, pg = pool_w.shape[1], pool_w.shape[2]
    assert ng == len(POOL_WINDOWS) and pg % V7X_LANES == 0
    assert max(POOL_WINDOWS) // 2 <= POOL_BLOCK and g.bseq % POOL_BLOCK == 0
    bm = g.bseq
    x_specs, x_args = _row_specs(g, x, (bm, pg), minor=lambda i, gi: gi, n_minor=ng)
    return pl.pallas_call(
        functools.partial(_pool_kernel, g=g),
        grid=(g.t // bm, ng),
        in_specs=[
            pl.BlockSpec((bm, pg), lambda i, gi: (i, gi)),
            pl.BlockSpec((None, None, pg, pg), lambda i, gi: (r, gi, 0, 0)),
            pl.BlockSpec((None, 1, pg), lambda i, gi: (r, 0, gi)),
        ] + x_specs + [_mod_spec(g, l, 2, bm, pg, col_axis=1)],
        out_specs=pl.BlockSpec((bm, pg), lambda i, gi: (i, gi)),
        out_shape=jax.ShapeDtypeStruct((g.t, g.d), F32),
        compiler_params=_params(2),
        name=f"pool_mixer_l{l}",
    )(h, pool_w, pool_scale.reshape(pool_scale.shape[0], 1, g.d), *x_args, mod)


def _ret_kernel(*refs, r, n_heads, seq, rope, has_s0, emit_state, fill_state, n_aliased, hps):
    it = iter(refs)
    dls_ref = next(it)
    q_ref, k_ref, v_ref, gate_ref = next(it), next(it), next(it), next(it)
    cos_ref, sin_ref = (next(it), next(it)) if rope else (None, None)
    s0_ref = next(it) if has_s0 else None
    for _ in range(n_aliased):
        next(it)
    o_ref = next(it)
    st_ref = next(it) if emit_state else None
    oacc_ref = next(it)

    hk = q_ref.shape[1] // hps
    hv = v_ref.shape[1] // hps
    hh = hk // 2
    n_chunks = seq // CHUNK

    if fill_state:
        for layer in range(st_ref.shape[0]):
            if layer != r:
                st_ref[layer] = jnp.zeros(st_ref.shape[1:], F32)
        st_ref = st_ref.at[r]

    ri = lax.broadcasted_iota(jnp.int32, (CHUNK, CHUNK), 0).astype(F32)
    ci = lax.broadcasted_iota(jnp.int32, (CHUNK, CHUNK), 1).astype(F32)
    diff = ri - ci
    idx = lax.broadcasted_iota(jnp.int32, (CHUNK, 1), 0).astype(F32)

    def rot(x, rows):
        parts = []
        for p in range(2):
            xp = x[:, p * hh:(p + 1) * hh]
            parts.append(xp * cos_ref[rows, p * hh:(p + 1) * hh]
                         + pltpu.roll(xp, hh // 2, axis=1) * sin_ref[rows, p * hh:(p + 1) * hh])
        return jnp.concatenate(parts, axis=1)

    def add_cross(o, qc, s, qdec):
        if s is None:
            return o
        return o + jnp.dot(qc, s.astype(BF16), preferred_element_type=F32) * qdec

    def advance(s, kf, kdec, cdec, vc):
        kd = (kf * kdec).astype(BF16)
        kv = lax.dot_general(kd, vc, (((0,), (0,)), ((), ())), preferred_element_type=F32)
        return kv if s is None else s * cdec + kv

    for hd in range(hps):
        head = pl.program_id(1) * hps + hd
        kcol = slice(hd * hk, (hd + 1) * hk)
        vcol = slice(hd * hv, (hd + 1) * hv)

        def log_gamma(direction):
            ls = dls_ref[(r * 2 + direction) * n_heads + head]
            return -jnp.exp(jnp.full((1, 1), ls, F32))

        lg_f, lg_b = log_gamma(0), log_gamma(1)
        intra = (jnp.where(diff >= 0, jnp.exp(lg_f * jnp.maximum(diff, 0.0)), 0.0)
                 + jnp.where(diff <= 0, jnp.exp(lg_b * jnp.maximum(-diff, 0.0)), 0.0))
        qdec_f = jnp.exp(lg_f * (idx + 1.0))
        kdec_f = jnp.exp(lg_f * (CHUNK - 1.0 - idx))
        qdec_b = jnp.exp(lg_b * (CHUNK - idx))
        kdec_b = jnp.exp(lg_b * idx)
        cdec_f = jnp.exp(lg_f * CHUNK)
        cdec_b = jnp.exp(lg_b * CHUNK)

        qs, kfs = [], []
        for c in range(n_chunks):
            rows = slice(c * CHUNK, (c + 1) * CHUNK)
            q = q_ref[rows, kcol]
            kf = k_ref[rows, kcol].astype(F32) * (hk ** -0.5)
            if rope:
                q = rot(q.astype(F32), rows).astype(BF16)
                kf = rot(kf, rows)
            qs.append(q)
            kfs.append(kf)

        s = s0_ref[0, hd] if has_s0 else None
        for c in range(n_chunks):
            rows = slice(c * CHUNK, (c + 1) * CHUNK)
            vc = v_ref[rows, vcol]
            scores = lax.dot_general(qs[c], kfs[c].astype(BF16), (((1,), (1,)), ((), ())),
                                     preferred_element_type=F32)
            o = jnp.dot((scores * intra).astype(BF16), vc, preferred_element_type=F32)
            oacc_ref[rows, vcol] = add_cross(o, qs[c], s, qdec_f)
            if emit_state or c + 1 < n_chunks:
                s = advance(s, kfs[c], kdec_f, cdec_f, vc)
        if emit_state:
            st_ref[0, hd] = s

        s = s0_ref[1, hd] if has_s0 else None
        for c in reversed(range(n_chunks)):
            rows = slice(c * CHUNK, (c + 1) * CHUNK)
            vc = v_ref[rows, vcol]
            o = add_cross(oacc_ref[rows, vcol], qs[c], s, qdec_b)
            mu = jnp.mean(o, axis=-1, keepdims=True)
            dev = o - mu
            var = jnp.mean(dev * dev, axis=-1, keepdims=True)
            normed = dev * lax.rsqrt(var + EPS)
            gate = gate_ref[rows, vcol].astype(F32)
            o_ref[rows, vcol] = (_silu(gate) * normed).astype(o_ref.dtype)
            if emit_state or c > 0:
                s = advance(s, kfs[c], kdec_b, cdec_b, vc)
        if emit_state:
            st_ref[1, hd] = s


def _retention(proj, decay_ls, r, n_heads, row0, n_seq, seq, rope_tables=None, s0=None,
               state_shape=None, state_buf=None):
    t, ret_in = proj.shape
    hk = ret_in // (6 * n_heads)
    hv = 2 * hk
    assert seq % CHUNK == 0 and row0 % seq == 0
    rb0 = row0 // seq
    rope = rope_tables is not None
    has_s0 = s0 is not None
    emit_state = state_shape is not None
    fill_state = emit_state and state_buf is None

    hps = RET_HEADS_PER_STEP
    assert n_heads % hps == 0
    nhb = n_heads // hps
    state_spec = pl.BlockSpec((None, None, 2, hps, hk, hv), lambda b, h, *_: (b, r, 0, h, 0, 0))

    in_specs = [
        pl.BlockSpec((seq, hps * hk), lambda b, h, *_: (rb0 + b, h)),
        pl.BlockSpec((seq, hps * hk), lambda b, h, *_: (rb0 + b, nhb + h)),
        pl.BlockSpec((seq, hps * hv), lambda b, h, *_: (rb0 + b, nhb + h)),
        pl.BlockSpec((seq, hps * hv), lambda b, h, *_: (rb0 + b, 2 * nhb + h)),
    ]
    args = [proj, proj, proj, proj]
    if rope:
        in_specs += [pl.BlockSpec((seq, hk), lambda b, h, *_: (0, 0))] * 2
        args += list(rope_tables)
    if has_s0:
        in_specs.append(state_spec)
        args.append(s0)
    aliases = {}
    out_specs = [pl.BlockSpec((seq, hps * hv), lambda b, h, *_: (b, h))]
    out_shape = [jax.ShapeDtypeStruct((n_seq * seq, n_heads * hv), BF16)]
    if emit_state:
        if fill_state:
            out_specs.append(pl.BlockSpec((None, state_shape[1], 2, hps, hk, hv),
                                          lambda b, h, *_: (b, 0, 0, h, 0, 0)))
        else:
            out_specs.append(state_spec)
            aliases[1 + len(args)] = 1
            in_specs.append(pl.BlockSpec(memory_space=pl.ANY))
            args.append(state_buf)
        out_shape.append(jax.ShapeDtypeStruct(state_shape, F32))

    out = pl.pallas_call(
        functools.partial(_ret_kernel, r=r, n_heads=n_heads, seq=seq, rope=rope, has_s0=has_s0,
                          emit_state=emit_state, fill_state=fill_state, n_aliased=len(aliases), hps=hps),
        grid_spec=pltpu.PrefetchScalarGridSpec(
            num_scalar_prefetch=1,
            grid=(n_seq, nhb),
            in_specs=in_specs,
            out_specs=out_specs,
            scratch_shapes=[pltpu.VMEM((seq, hps * hv), F32)],
        ),
        out_shape=out_shape,
        input_output_aliases=aliases,
        compiler_params=_params(2),
        name=f"retention_r{r}_{'latent' if rope else 'prompt'}",
    )(decay_ls.reshape(-1), *args)
    return out if emit_state else (out[0], None)


def _rope_tables(seq, hk):
    rows = seq // GRID_W
    row_ids = jnp.repeat(jnp.arange(rows), GRID_W).astype(F32)
    col_ids = jnp.tile(jnp.arange(GRID_W), rows).astype(F32)
    n = hk // 4
    freqs = ROPE_BASE ** (-jnp.arange(n, dtype=F32) / n)
    cos, sin = [], []
    for ang in (row_ids[:, None] * freqs, col_ids[:, None] * freqs):
        c, s = jnp.cos(ang), jnp.sin(ang)
        cos += [c, c]
        sin += [-s, s]
    return jnp.concatenate(cos, axis=1), jnp.concatenate(sin, axis=1)


def kernel(x_prompt, x_sample, state_ret, c, c_ctx, mod_w, mod_b, norm_g, final_norm_g, pool_w, pool_scale,
           ret_w_in, ret_w_out, ret_decay_ls, ffn_w_up, ffn_conv_w, ffn_conv_b, ffn_w_down):
    b, s, d = x_prompt.shape
    db, ds, _ = x_sample.shape
    depth = mod_w.shape[0]
    n_heads = ret_decay_ls.shape[-1]
    n_ret = ret_w_in.shape[0]
    hk = d // n_heads
    hv = ret_w_out.shape[1] // n_heads
    nf = ffn_w_down.shape[1]
    nfp = -(-nf // FF_ALIGN) * FF_ALIGN

    bseq = max(s, ds)
    g = Geom(tp=b * s, ts=db * ds, s=s, ds=ds, d=d, bseq=bseq)
    assert bseq % s == 0 and bseq % ds == 0 and g.tp % bseq == 0 and g.ts % bseq == 0
    assert g.tp % ds == 0 and g.tp % ROW_TILE == 0 and ds % ROW_TILE == 0
    assert 1 + db <= N_COND_ROWS and hv == 2 * hk and ds % GRID_W == 0

    pool_w_b = _to_bf16(pool_w.reshape((-1,) + pool_w.shape[2:])).reshape(pool_w.shape)
    w_out_b = _to_bf16(ret_w_out)
    w_down_b = _to_bf16(ffn_w_down, nfp)
    conv_b = ffn_conv_b.reshape(depth, 1, 2 * nf)

    cond = jnp.concatenate([c_ctx[None, :], c, jnp.zeros((N_COND_ROWS - 1 - db, d), F32)], axis=0)
    mod = _modulation(cond, mod_w, mod_b)

    x = (x_prompt.reshape(g.tp, d), x_sample.reshape(g.ts, d))
    rope_tables = _rope_tables(ds, hk)
    state_shape = (b, n_ret, 2, n_heads, hk, hv)
    state_buf = None

    for l in range(depth):
        r = l // N_MIXERS
        if l % N_MIXERS == 0:
            h = _norm_mod(g, x, norm_g, mod, l, 0, F32)
            x = _pool_mixer(g, h, pool_w_b, pool_scale, x, mod, l, r)
        else:
            h = _norm_mod(g, x, norm_g, mod, l, 0, BF16)
            proj = _matmul(h, ret_w_in, r, BF16, f"ret_in_l{l}")
            o_prompt, state_buf = _retention(proj, ret_decay_ls, r, n_heads, 0, b, s,
                                             state_shape=state_shape, state_buf=state_buf)
            o_latent, _ = _retention(proj, ret_decay_ls, r, n_heads, g.tp, db, ds,
                                     rope_tables=rope_tables, s0=state_ret)
            x = _matmul_residual(g, (o_prompt, o_latent), w_out_b, r, x, mod, l, 2, f"ret_out_l{l}")
        h = _norm_mod(g, x, norm_g, mod, l, 1, BF16)
        act = _ffn_up(g, h, ffn_w_up, ffn_conv_w, conv_b, l, nfp)
        x = _matmul_residual(g, act, w_down_b, l, x, mod, l, 5, f"ffn_down_l{l}")

    y_prompt = _final_norm(g, x, final_norm_g, 0, g.tp).reshape(b, s, d)
    y_sample = _final_norm(g, x, final_norm_g, g.tp, g.ts).reshape(db, ds, d)
    return y_prompt, y_sample, state_buf
```
